```python
import math
import jax, jax.numpy as jnp
from jax import lax
import numpy as np

D_MODEL = 2048
BATCH = 4
SEQ = 4096
DEPTH = 2

EPS = 1e-6
N_BRANCHES = 3
C_CONV = D_MODEL // 2
CONV_WIDTH = 31
MLA_HEADS = 8
NOPE_DIM = 128
ROPE_DIM = 64
V_DIM = 128
Q_LORA = D_MODEL // 4
KV_LORA = D_MODEL // 8
ROPE_BASE = 10000.0
MAX_POS_OFFSET = 1024
Q_BLOCK = 128
MLSTM_HEADS = 4
D_MLSTM = D_MODEL // 2
MLSTM_HEAD_DIM = D_MLSTM // MLSTM_HEADS
MLSTM_CONV_WIDTH = 4
MLSTM_CHUNK = 64
N_MEM = 256
XATTN_HEADS = 4
XATTN_HEAD_DIM = 128
FFN_HIDDEN = ((8 * D_MODEL + 3 * 256 - 1) // (3 * 256)) * 256
IN_COLS = 2 * C_CONV + Q_LORA + KV_LORA + ROPE_DIM + 2 * D_MLSTM + N_BRANCHES * D_MODEL

kernel_name = 'hybrid_conformer_mla_mlstm_block'


def rms_norm(x, g, eps=EPS):
    xf = x.astype(jnp.float32)
    y = xf * lax.rsqrt(jnp.mean(xf * xf, axis=-1, keepdims=True) + eps)
    return (y * g.astype(jnp.float32)).astype(x.dtype)


def layer_norm(x, g, b, eps=EPS):
    xf = x.astype(jnp.float32)
    mu = jnp.mean(xf, axis=-1, keepdims=True)
    var = jnp.mean(jnp.square(xf - mu), axis=-1, keepdims=True)
    y = (xf - mu) * lax.rsqrt(var + eps)
    return (y * g.astype(jnp.float32) + b.astype(jnp.float32)).astype(x.dtype)


def causal_depthwise_conv(x, w, b):
    width = w.shape[0]
    y = lax.conv_general_dilated(x, w[:, None, :], window_strides=(1,), padding=[(width - 1, 0)],
                                 dimension_numbers=('NWC', 'WIO', 'NWC'),
                                 feature_group_count=x.shape[-1])
    return y + b


def rope(x, positions):
    half = x.shape[-1] // 2
    inv_freq = ROPE_BASE ** (-jnp.arange(half, dtype=jnp.float32) / half)
    ang = positions.astype(jnp.float32)[:, :, None, None] * inv_freq
    cos, sin = jnp.cos(ang), jnp.sin(ang)
    xf = x.astype(jnp.float32)
    x1, x2 = xf[..., :half], xf[..., half:]
    return jnp.concatenate([x1 * cos - x2 * sin, x1 * sin + x2 * cos], axis=-1).astype(x.dtype)


def causal_block_attention(q, k, v, scale):
    B, S, H, Dq = q.shape
    nb = S // Q_BLOCK
    qb = q.reshape(B, nb, Q_BLOCK, H, Dq).transpose(1, 0, 2, 3, 4)
    kpos = jnp.arange(S)

    def one_block(args):
        q_blk, i = args
        s = jnp.einsum('bqhd,bkhd->bhqk', q_blk, k).astype(jnp.float32) * scale
        qpos = i * Q_BLOCK + jnp.arange(Q_BLOCK)
        s = jnp.where(kpos[None, :] <= qpos[:, None], s, -jnp.inf)
        p = jax.nn.softmax(s, axis=-1).astype(v.dtype)
        return jnp.einsum('bhqk,bkhd->bqhd', p, v)

    out = lax.map(one_block, (qb, jnp.arange(nb)))
    return out.transpose(1, 0, 2, 3, 4).reshape(B, S, H, v.shape[-1])


def conformer_conv_branch(conv_in, conv_dw, conv_dw_b, conv_ln_g, conv_ln_b):
    a, g = jnp.split(conv_in, 2, axis=-1)
    u = a * jax.nn.sigmoid(g)
    u = causal_depthwise_conv(u, conv_dw, conv_dw_b)
    return jax.nn.silu(layer_norm(u, conv_ln_g, conv_ln_b))


def mla_branch(cq, ckv, k_rope, positions, q_norm_g, kv_norm_g, w_q_up, w_kv_up, g_q, g_k):
    B, S, _ = cq.shape
    q = (rms_norm(cq, q_norm_g) @ w_q_up).reshape(B, S, MLA_HEADS, NOPE_DIM + ROPE_DIM)
    kv = (rms_norm(ckv, kv_norm_g) @ w_kv_up).reshape(B, S, MLA_HEADS, NOPE_DIM + V_DIM)
    k_nope, v = kv[..., :NOPE_DIM], kv[..., NOPE_DIM:]
    q_nope = rms_norm(q[..., :NOPE_DIM], g_q[:NOPE_DIM])
    q_rot = rope(rms_norm(q[..., NOPE_DIM:], g_q[NOPE_DIM:]), positions)
    k_nope = rms_norm(k_nope, g_k[:NOPE_DIM])
    k_rot = rope(rms_norm(k_rope, g_k[NOPE_DIM:])[:, :, None, :], positions)
    q = jnp.concatenate([q_nope, q_rot], axis=-1)
    k = jnp.concatenate([k_nope, jnp.broadcast_to(k_rot, (B, S, MLA_HEADS, ROPE_DIM))], axis=-1)
    o = causal_block_attention(q, k, v, (NOPE_DIM + ROPE_DIM) ** -0.5)
    return o.reshape(B, S, MLA_HEADS * V_DIM)


def mlstm_chunkwise(q, k, v, log_i, log_f):
    B, S, H, DH = q.shape
    L = MLSTM_CHUNK
    NC = S // L

    def to_chunks(t):
        return t.astype(jnp.float32).reshape(B, NC, L, H, DH).transpose(1, 0, 3, 2, 4)

    def gate_chunks(t):
        return t.astype(jnp.float32).reshape(B, NC, L, H).transpose(1, 0, 3, 2)

    qc, kc, vc = to_chunks(q), to_chunks(k) * (DH ** -0.5), to_chunks(v)
    lic, lfc = gate_chunks(log_i), gate_chunks(log_f)
    causal = jnp.tril(jnp.ones((L, L), dtype=bool))

    def step(carry, xs):
        C, n, m = carry
        qj, kj, vj, li, lf = xs
        b = jnp.cumsum(lf, axis=-1)
        Dlog = b[..., :, None] - b[..., None, :] + li[..., None, :]
        Dlog = jnp.where(causal, Dlog, -jnp.inf)
        inter = b + m[..., None]
        m_comb = jnp.maximum(inter, jnp.max(Dlog, axis=-1))
        w_intra = jnp.exp(Dlog - m_comb[..., None])
        w_inter = jnp.exp(inter - m_comb)
        s = jnp.einsum('bhid,bhjd->bhij', qj, kj) * w_intra
        num = w_inter[..., None] * jnp.einsum('bhid,bhde->bhie', qj, C) + jnp.einsum('bhij,bhje->bhie', s, vj)
        den = w_inter * jnp.einsum('bhid,bhd->bhi', qj, n) + jnp.sum(s, axis=-1)
        h = num / jnp.maximum(jnp.abs(den), jnp.exp(-m_comb))[..., None]
        bL = b[..., -1]
        g = bL[..., None] - b + li
        m_new = jnp.maximum(bL + m, jnp.max(g, axis=-1))
        wg = jnp.exp(g - m_new[..., None])
        decay = jnp.exp(bL + m - m_new)
        C_new = decay[..., None, None] * C + jnp.einsum('bhs,bhsd,bhse->bhde', wg, kj, vj)
        n_new = decay[..., None] * n + jnp.einsum('bhs,bhsd->bhd', wg, kj)
        return (C_new, n_new, m_new), h

    init = (jnp.zeros((B, H, DH, DH), jnp.float32), jnp.zeros((B, H, DH), jnp.float32),
            jnp.zeros((B, H), jnp.float32))
    _, hs = lax.scan(step, init, (qc, kc, vc, lic, lfc))
    return hs.transpose(1, 0, 3, 2, 4).reshape(B, S, H, DH).astype(q.dtype)


def mlstm_branch(xm, z, conv_w, conv_b, w_mq, w_mk, w_mv, w_if, b_if, gn_g, skip):
    B, S, _ = xm.shape
    xc = jax.nn.silu(causal_depthwise_conv(xm, conv_w, conv_b))
    xch = xc.reshape(B, S, MLSTM_HEADS, MLSTM_HEAD_DIM)
    xmh = xm.reshape(B, S, MLSTM_HEADS, MLSTM_HEAD_DIM)
    q = jnp.einsum('bshd,hde->bshe', xch, w_mq)
    k = jnp.einsum('bshd,hde->bshe', xch, w_mk)
    v = jnp.einsum('bshd,hde->bshe', xmh, w_mv)
    qkv = jnp.concatenate([q.reshape(B, S, D_MLSTM), k.reshape(B, S, D_MLSTM), v.reshape(B, S, D_MLSTM)], axis=-1)
    if_pre = (qkv @ w_if + b_if).astype(jnp.float32)
    log_i = if_pre[..., :MLSTM_HEADS]
    log_f = jax.nn.log_sigmoid(if_pre[..., MLSTM_HEADS:])
    h = mlstm_chunkwise(q, k, v, log_i, log_f)
    hf = h.astype(jnp.float32)
    mu = jnp.mean(hf, axis=-1, keepdims=True)
    var = jnp.mean(jnp.square(hf - mu), axis=-1, keepdims=True)
    hn = ((hf - mu) * lax.rsqrt(var + EPS) * gn_g.reshape(MLSTM_HEADS, MLSTM_HEAD_DIM).astype(jnp.float32)).astype(xm.dtype)
    hn = hn.reshape(B, S, D_MLSTM) + skip * xc
    return hn * jax.nn.silu(z)


def hybrid_mixer(h, positions, w_in, b_gate, conv_dw, conv_dw_b, conv_ln_g, conv_ln_b, w_conv_out,
                 mla_q_norm, mla_kv_norm, w_q_up, w_kv_up, mla_g_q, mla_g_k, w_mla_out,
                 mlstm_conv_w, mlstm_conv_b, w_mq, w_mk, w_mv, w_if, b_if, mlstm_gn_g, mlstm_skip,
                 w_mlstm_out, w_mix_out):
    B, S, _ = h.shape
    proj = h @ w_in
    sizes = (2 * C_CONV, Q_LORA, KV_LORA, ROPE_DIM, D_MLSTM, D_MLSTM, N_BRANCHES * D_MODEL)
    conv_in, cq, ckv, k_rope, xm, z, gate_pre = jnp.split(proj, np.cumsum(sizes)[:-1].tolist(), axis=-1)
    gates = jax.nn.sigmoid(gate_pre + b_gate).reshape(B, S, N_BRANCHES, D_MODEL)
    y_conv = conformer_conv_branch(conv_in, conv_dw, conv_dw_b, conv_ln_g, conv_ln_b) @ w_conv_out
    y_mla = mla_branch(cq, ckv, k_rope, positions, mla_q_norm, mla_kv_norm, w_q_up, w_kv_up,
                       mla_g_q, mla_g_k) @ w_mla_out
    y_mlstm = mlstm_branch(xm, z, mlstm_conv_w, mlstm_conv_b, w_mq, w_mk, w_mv, w_if, b_if,
                           mlstm_gn_g, mlstm_skip) @ w_mlstm_out
    merged = gates[:, :, 0] * y_conv + gates[:, :, 1] * y_mla + gates[:, :, 2] * y_mlstm
    return merged @ w_mix_out


def memory_cross_attention(h, mem_n, w_xq, w_xkv, g_q, g_k, w_xo):
    B, S, _ = h.shape
    M = mem_n.shape[1]
    q = rms_norm((h @ w_xq).reshape(B, S, XATTN_HEADS, XATTN_HEAD_DIM), g_q)
    kv = (mem_n @ w_xkv).reshape(B, M, 2, XATTN_HEADS, XATTN_HEAD_DIM)
    k = rms_norm(kv[:, :, 0], g_k)
    v = kv[:, :, 1]
    s = jnp.einsum('bshd,bmhd->bhsm', q, k).astype(jnp.float32) * (XATTN_HEAD_DIM ** -0.5)
    p = jax.nn.softmax(s, axis=-1).astype(v.dtype)
    o = jnp.einsum('bhsm,bmhd->bshd', p, v).reshape(B, S, XATTN_HEADS * XATTN_HEAD_DIM)
    return o @ w_xo


def swiglu_ffn(h, w_ffn_in, w_ffn_out):
    gate, up = jnp.split(h @ w_ffn_in, 2, axis=-1)
    return (jax.nn.silu(gate) * up) @ w_ffn_out


def setup_inputs(seed: int = 0) -> dict:
    key = jax.random.key(seed)
    ks = iter(jax.random.split(key, 64))
    f32 = jnp.float32
    Lr = DEPTH

    def w(shape, fan_in):
        return jax.random.normal(next(ks), shape, f32) * (fan_in ** -0.5)

    def gain(shape):
        return 1.0 + 0.02 * jax.random.normal(next(ks), shape, f32)

    def bias(shape, scale=0.02):
        return scale * jax.random.normal(next(ks), shape, f32)

    x = jax.random.normal(next(ks), (BATCH, SEQ, D_MODEL), f32)
    mem = jax.random.normal(next(ks), (BATCH, N_MEM, D_MODEL), f32)
    positions = (jax.random.randint(next(ks), (BATCH, 1), 0, MAX_POS_OFFSET, dtype=jnp.int32)
                 + jnp.arange(SEQ, dtype=jnp.int32)[None, :])
    norm_mix = gain((Lr, D_MODEL))
    w_in = w((Lr, D_MODEL, IN_COLS), D_MODEL)
    b_gate = bias((Lr, N_BRANCHES * D_MODEL))
    conv_dw = w((Lr, CONV_WIDTH, C_CONV), CONV_WIDTH)
    conv_dw_b = bias((Lr, C_CONV))
    conv_ln_g = gain((Lr, C_CONV))
    conv_ln_b = bias((Lr, C_CONV))
    w_conv_out = w((Lr, C_CONV, D_MODEL), C_CONV)
    mla_q_norm = gain((Lr, Q_LORA))
    mla_kv_norm = gain((Lr, KV_LORA))
    w_q_up = w((Lr, Q_LORA, MLA_HEADS * (NOPE_DIM + ROPE_DIM)), Q_LORA)
    w_kv_up = w((Lr, KV_LORA, MLA_HEADS * (NOPE_DIM + V_DIM)), KV_LORA)
    mla_g_q = gain((Lr, NOPE_DIM + ROPE_DIM))
    mla_g_k = gain((Lr, NOPE_DIM + ROPE_DIM))
    w_mla_out = w((Lr, MLA_HEADS * V_DIM, D_MODEL), MLA_HEADS * V_DIM)
    mlstm_conv_w = w((Lr, MLSTM_CONV_WIDTH, D_MLSTM), MLSTM_CONV_WIDTH)
    mlstm_conv_b = bias((Lr, D_MLSTM))
    w_mq = w((Lr, MLSTM_HEADS, MLSTM_HEAD_DIM, MLSTM_HEAD_DIM), MLSTM_HEAD_DIM)
    w_mk = w((Lr, MLSTM_HEADS, MLSTM_HEAD_DIM, MLSTM_HEAD_DIM), MLSTM_HEAD_DIM)
    w_mv = w((Lr, MLSTM_HEADS, MLSTM_HEAD_DIM, MLSTM_HEAD_DIM), MLSTM_HEAD_DIM)
    w_if = w((Lr, 3 * D_MLSTM, 2 * MLSTM_HEADS), 3 * D_MLSTM)
    b_i = bias((Lr, MLSTM_HEADS), 0.1)
    b_f = jnp.linspace(3.0, 6.0, MLSTM_HEADS, dtype=f32)[None, :] + bias((Lr, MLSTM_HEADS), 0.1)
    b_if = jnp.concatenate([b_i, b_f], axis=-1)
    mlstm_gn_g = gain((Lr, D_MLSTM))
    mlstm_skip = gain((Lr, D_MLSTM))
    w_mlstm_out = w((Lr, D_MLSTM, D_MODEL), D_MLSTM)
    w_mix_out = w((Lr, D_MODEL, D_MODEL), D_MODEL)
    norm_x = gain((Lr, D_MODEL))
    norm_mem = gain((Lr, D_MODEL))
    w_xq = w((Lr, D_MODEL, XATTN_HEADS * XATTN_HEAD_DIM), D_MODEL)
    w_xkv = w((Lr, D_MODEL, 2 * XATTN_HEADS * XATTN_HEAD_DIM), D_MODEL)
    xattn_g_q = gain((Lr, XATTN_HEAD_DIM))
    xattn_g_k = gain((Lr, XATTN_HEAD_DIM))
    w_xo = w((Lr, XATTN_HEADS * XATTN_HEAD_DIM, D_MODEL), XATTN_HEADS * XATTN_HEAD_DIM)
    norm_ffn = gain((Lr, D_MODEL))
    w_ffn_in = w((Lr, D_MODEL, 2 * FFN_HIDDEN), D_MODEL)
    w_ffn_out = w((Lr, FFN_HIDDEN, D_MODEL), FFN_HIDDEN)
    return {'x': x, 'mem': mem, 'positions': positions, 'norm_mix': norm_mix, 'w_in': w_in,
            'b_gate': b_gate, 'conv_dw': conv_dw, 'conv_dw_b': conv_dw_b, 'conv_ln_g': conv_ln_g,
            'conv_ln_b': conv_ln_b, 'w_conv_out': w_conv_out, 'mla_q_norm': mla_q_norm,
            'mla_kv_norm': mla_kv_norm, 'w_q_up': w_q_up, 'w_kv_up': w_kv_up, 'mla_g_q': mla_g_q,
            'mla_g_k': mla_g_k, 'w_mla_out': w_mla_out, 'mlstm_conv_w': mlstm_conv_w,
            'mlstm_conv_b': mlstm_conv_b, 'w_mq': w_mq, 'w_mk': w_mk, 'w_mv': w_mv, 'w_if': w_if,
            'b_if': b_if, 'mlstm_gn_g': mlstm_gn_g, 'mlstm_skip': mlstm_skip,
            'w_mlstm_out': w_mlstm_out, 'w_mix_out': w_mix_out, 'norm_x': norm_x,
            'norm_mem': norm_mem, 'w_xq': w_xq, 'w_xkv': w_xkv, 'xattn_g_q': xattn_g_q,
            'xattn_g_k': xattn_g_k, 'w_xo': w_xo, 'norm_ffn': norm_ffn, 'w_ffn_in': w_ffn_in,
            'w_ffn_out': w_ffn_out}


def reference(x, mem, positions, norm_mix, w_in, b_gate, conv_dw, conv_dw_b, conv_ln_g, conv_ln_b,
              w_conv_out, mla_q_norm, mla_kv_norm, w_q_up, w_kv_up, mla_g_q, mla_g_k, w_mla_out,
              mlstm_conv_w, mlstm_conv_b, w_mq, w_mk, w_mv, w_if, b_if, mlstm_gn_g, mlstm_skip,
              w_mlstm_out, w_mix_out, norm_x, norm_mem, w_xq, w_xkv, xattn_g_q, xattn_g_k, w_xo,
              norm_ffn, w_ffn_in, w_ffn_out):
    for l in range(DEPTH):
        h = rms_norm(x, norm_mix[l])
        x = x + hybrid_mixer(h, positions, w_in[l], b_gate[l], conv_dw[l], conv_dw_b[l], conv_ln_g[l],
                             conv_ln_b[l], w_conv_out[l], mla_q_norm[l], mla_kv_norm[l], w_q_up[l],
                             w_kv_up[l], mla_g_q[l], mla_g_k[l], w_mla_out[l], mlstm_conv_w[l],
                             mlstm_conv_b[l], w_mq[l], w_mk[l], w_mv[l], w_if[l], b_if[l],
                             mlstm_gn_g[l], mlstm_skip[l], w_mlstm_out[l], w_mix_out[l])
        h = rms_norm(x, norm_x[l])
        mem_n = rms_norm(mem, norm_mem[l])
        x = x + memory_cross_attention(h, mem_n, w_xq[l], w_xkv[l], xattn_g_q[l], xattn_g_k[l], w_xo[l])
        h = rms_norm(x, norm_ffn[l])
        x = x + swiglu_ffn(h, w_ffn_in[l], w_ffn_out[l])
    return x
```

```python
import functools

import jax
import jax.numpy as jnp
import numpy as np
from jax import lax
from jax.experimental import pallas as pl
from jax.experimental.pallas import tpu as pltpu

F32 = jnp.float32
BF16 = jnp.bfloat16

EPS = 1e-6
CONV_WIDTH = 31
MLA_HEADS = 8
NOPE_DIM = 128
ROPE_DIM = 64
V_DIM = 128
ROPE_BASE = 10000.0
MLSTM_HEADS = 4
MLSTM_CONV_WIDTH = 4
XATTN_HEADS = 4
XATTN_HEAD_DIM = 128
LANES = 128
NEG_BIG = -1e30
VMEM_LIMIT = 56 * 1024 * 1024

MLSTM_CHUNK = 256
CONV_HALO = 32
MLSTM_HALO = 16


def _params(*sem):
    return pltpu.CompilerParams(dimension_semantics=sem, vmem_limit_bytes=VMEM_LIMIT)


def _sigmoid(x):
    return 1.0 / (1.0 + jnp.exp(-x))


def _silu(x):
    return x * _sigmoid(x)


def _dot(a, b):
    return jnp.dot(a, b, preferred_element_type=F32)


def _dot_nt(a, b):
    return lax.dot_general(a, b, (((1,), (1,)), ((), ())), preferred_element_type=F32)


def _rms(x, n):
    ms = jnp.sum(x * x, axis=-1, keepdims=True) * (1.0 / n)
    return x * lax.rsqrt(ms + EPS)


def _mm_kernel(*refs, norm, swiglu, has_res):
    it = iter(refs)
    x_ref = next(it)
    g_ref = next(it) if norm else None
    w_ref = next(it)
    w2_ref = next(it) if swiglu else None
    r_ref = next(it) if has_res else None
    o_ref = next(it)
    h_ref = next(it) if norm else None
    if norm:
        @pl.when(pl.program_id(1) == 0)
        def _():
            xf = x_ref[...].astype(F32)
            h_ref[...] = (_rms(xf, xf.shape[-1]) * g_ref[...]).astype(BF16)
        h = h_ref[...]
    else:
        h = x_ref[...]
    acc = _dot(h, w_ref[...])
    if swiglu:
        acc = _silu(acc) * _dot(h, w2_ref[...])
    if has_res:
        acc = r_ref[...] + acc
    o_ref[...] = acc.astype(o_ref.dtype)


def _matmul(x, w, *, bm, bn, gain=None, swiglu=False, residual=None, out_dtype=BF16,
            x_cols=None, name="matmul"):
    M = x.shape[0]
    c0, K = (0, x.shape[1]) if x_cols is None else x_cols
    assert c0 % K == 0 and w.shape[0] == K
    N = w.shape[1] // 2 if swiglu else w.shape[1]
    bm, bn = min(bm, M), min(bn, N)
    assert M % bm == 0 and N % bn == 0
    xcb = c0 // K
    norm = gain is not None
    in_specs = [pl.BlockSpec((bm, K), lambda i, j: (i, xcb))]
    args = [x]
    if norm:
        in_specs.append(pl.BlockSpec((1, K), lambda i, j: (0, 0)))
        args.append(gain.reshape(1, K).astype(F32))
    in_specs.append(pl.BlockSpec((K, bn), lambda i, j: (0, j)))
    args.append(w)
    if swiglu:
        nb = N // bn
        in_specs.append(pl.BlockSpec((K, bn), lambda i, j: (0, j + nb)))
        args.append(w)
    if residual is not None:
        in_specs.append(pl.BlockSpec((bm, bn), lambda i, j: (i, j)))
        args.append(residual)
    scratch = [pltpu.VMEM((bm, K), BF16)] if norm else []
    return pl.pallas_call(
        functools.partial(_mm_kernel, norm=norm, swiglu=swiglu, has_res=residual is not None),
        grid=(M // bm, N // bn),
        in_specs=in_specs,
        out_specs=pl.BlockSpec((bm, bn), lambda i, j: (i, j)),
        out_shape=jax.ShapeDtypeStruct((M, N), out_dtype),
        scratch_shapes=scratch,
        compiler_params=_params("parallel", "arbitrary"),
        name=name,
    )(*args)


def _conv_kernel(a_ref, g_ref, w_ref, b_ref, lg_ref, lb_ref, o_ref, u_ref, *, bt, rt):
    i = pl.program_id(1)

    @pl.when(i == 0)
    def _():
        u_ref[0:CONV_HALO, :] = jnp.zeros((CONV_HALO, u_ref.shape[1]), F32)

    @pl.when(i > 0)
    def _():
        u_ref[0:CONV_HALO, :] = u_ref[bt:bt + CONV_HALO, :]

    a = a_ref[...].astype(F32)
    g = g_ref[...].astype(F32)
    u_ref[CONV_HALO:CONV_HALO + bt, :] = a * _sigmoid(g)
    bias = b_ref[...]
    off = CONV_HALO - (CONV_WIDTH - 1)
    for r0 in range(0, bt, rt):
        acc = jnp.broadcast_to(bias, (rt, bias.shape[1]))
        for j in range(CONV_WIDTH):
            acc = acc + w_ref[j:j + 1, :] * u_ref[r0 + off + j:r0 + off + j + rt, :]
        mu = jnp.mean(acc, axis=-1, keepdims=True)
        d = acc - mu
        var = jnp.mean(d * d, axis=-1, keepdims=True)
        y = d * lax.rsqrt(var + EPS) * lg_ref[...] + lb_ref[...]
        o_ref[r0:r0 + rt, :] = _silu(y).astype(o_ref.dtype)


def _conv_branch(proj, col_a, col_g, C, B, S, w, b, ln_g, ln_b, *, bt=256, rt=32):
    bt = min(bt, S)
    rt = min(rt, bt)
    nt = S // bt
    ca, cg = col_a // C, col_g // C
    return pl.pallas_call(
        functools.partial(_conv_kernel, bt=bt, rt=rt),
        grid=(B, nt),
        in_specs=[
            pl.BlockSpec((bt, C), lambda b_, i: (b_ * nt + i, ca)),
            pl.BlockSpec((bt, C), lambda b_, i: (b_ * nt + i, cg)),
            pl.BlockSpec((CONV_WIDTH, C), lambda b_, i: (0, 0)),
            pl.BlockSpec((1, C), lambda b_, i: (0, 0)),
            pl.BlockSpec((1, C), lambda b_, i: (0, 0)),
            pl.BlockSpec((1, C), lambda b_, i: (0, 0)),
        ],
        out_specs=pl.BlockSpec((bt, C), lambda b_, i: (b_ * nt + i, 0)),
        out_shape=jax.ShapeDtypeStruct((B * S, C), BF16),
        scratch_shapes=[pltpu.VMEM((CONV_HALO + bt, C), F32)],
        compiler_params=_params("parallel", "arbitrary"),
        name="conformer_conv",
    )(proj, proj, w, b.reshape(1, C), ln_g.reshape(1, C), ln_b.reshape(1, C))


def _mla_prep_kernel(cq_ref, ckv_ref, kr_ref, krs_ref, pos_ref, freq_ref, sgn_ref,
                     qn_ref, kvn_ref, wq_ref, wkv_ref, gq_ref, gk_ref,
                     q_ref, k_ref, v_ref, *, scale):
    H = MLA_HEADS
    pos = pos_ref[...].astype(F32)
    ang = pos * freq_ref[...]
    cs1 = jnp.cos(ang)
    cs2 = jnp.sin(ang) * sgn_ref[...]

    cq = cq_ref[...].astype(F32)
    hq = (_rms(cq, cq.shape[-1]) * qn_ref[...]).astype(BF16)
    qall = _dot(hq, wq_ref[...])
    ckv = ckv_ref[...].astype(F32)
    hkv = (_rms(ckv, ckv.shape[-1]) * kvn_ref[...]).astype(BF16)
    kvall = _dot(hkv, wkv_ref[...])

    g_nope_q, g_rot_q, g_rots_q = gq_ref[0:1, :], gq_ref[1:2, :], gq_ref[2:3, :]
    g_nope_k, g_rot_k, g_rots_k = gk_ref[0:1, :], gk_ref[1:2, :], gk_ref[2:3, :]

    kr = kr_ref[...].astype(F32)
    krs = krs_ref[...].astype(F32)
    rs = lax.rsqrt(jnp.sum(kr * kr, axis=-1, keepdims=True) * (1.0 / ROPE_DIM) + EPS)
    k_rot = ((kr * rs * g_rot_k) * cs1 + (krs * rs * g_rots_k) * cs2).astype(BF16)

    for h in range(H):
        qn = qall[:, h * LANES:(h + 1) * LANES]
        q_ref[h, :, 0:LANES] = (_rms(qn, NOPE_DIM) * g_nope_q * scale).astype(BF16)
        qr = qall[:, (H + h) * LANES:(H + h + 1) * LANES]
        qs = qall[:, (2 * H + h) * LANES:(2 * H + h + 1) * LANES]
        rq = lax.rsqrt(jnp.sum(qr * qr, axis=-1, keepdims=True) * (1.0 / ROPE_DIM) + EPS)
        q_rot = (qr * rq * g_rot_q) * cs1 + (qs * rq * g_rots_q) * cs2
        q_ref[h, :, LANES:2 * LANES] = (q_rot * scale).astype(BF16)
        kn = kvall[:, h * LANES:(h + 1) * LANES]
        k_ref[h, :, 0:LANES] = (_rms(kn, NOPE_DIM) * g_nope_k).astype(BF16)
        k_ref[h, :, LANES:2 * LANES] = k_rot
        v_ref[h, :, :] = kvall[:, (H + h) * LANES:(H + h + 1) * LANES].astype(BF16)


def _pad_lanes(v, n=LANES):
    return jnp.pad(v, [(0, 0)] * (v.ndim - 1) + [(0, n - v.shape[-1])])


def _swap_halves(v):
    half = v.shape[-1] // 2
    return jnp.concatenate([v[..., half:], v[..., :half]], axis=-1)


def _mla_prep(proj, cols, pos, mla_q_norm, mla_kv_norm, w_q_up, w_kv_up, g_q, g_k, *, bm=512):
    T = proj.shape[0]
    bm = min(bm, T)
    H = MLA_HEADS
    q_lora, kv_lora = w_q_up.shape[0], w_kv_up.shape[0]
    c_cq, c_ckv, c_kr, c_krs = cols
    half = ROPE_DIM // 2
    inv_freq = ROPE_BASE ** (-jnp.arange(half, dtype=F32) / half)
    freq = _pad_lanes(jnp.concatenate([inv_freq, inv_freq]))[None, :]
    sgn = _pad_lanes(jnp.concatenate([-jnp.ones((half,), F32), jnp.ones((half,), F32)]))[None, :]
    wq = w_q_up.reshape(q_lora, H, NOPE_DIM + ROPE_DIM)
    wq_rot = wq[:, :, NOPE_DIM:]
    wq_p = jnp.concatenate([
        wq[:, :, :NOPE_DIM].reshape(q_lora, H * NOPE_DIM),
        _pad_lanes(wq_rot).reshape(q_lora, H * LANES),
        _pad_lanes(_swap_halves(wq_rot)).reshape(q_lora, H * LANES)], axis=1).astype(BF16)
    wkv = w_kv_up.reshape(kv_lora, H, NOPE_DIM + V_DIM)
    wkv_p = jnp.concatenate([wkv[:, :, :NOPE_DIM].reshape(kv_lora, H * NOPE_DIM),
                             wkv[:, :, NOPE_DIM:].reshape(kv_lora, H * V_DIM)], axis=1).astype(BF16)

    def gains(g):
        return jnp.stack([g[:NOPE_DIM], _pad_lanes(g[NOPE_DIM:]), _pad_lanes(_swap_halves(g[NOPE_DIM:]))])

    scale = float((NOPE_DIM + ROPE_DIM) ** -0.5)
    const = lambda i: (0, 0)
    return pl.pallas_call(
        functools.partial(_mla_prep_kernel, scale=scale),
        grid=(T // bm,),
        in_specs=[
            pl.BlockSpec((bm, q_lora), lambda i: (i, c_cq // q_lora)),
            pl.BlockSpec((bm, kv_lora), lambda i: (i, c_ckv // kv_lora)),
            pl.BlockSpec((bm, LANES), lambda i: (i, c_kr // LANES)),
            pl.BlockSpec((bm, LANES), lambda i: (i, c_krs // LANES)),
            pl.BlockSpec((bm, 1), lambda i: (i, 0)),
            pl.BlockSpec((1, LANES), const),
            pl.BlockSpec((1, LANES), const),
            pl.BlockSpec((1, q_lora), const),
            pl.BlockSpec((1, kv_lora), const),
            pl.BlockSpec(wq_p.shape, const),
            pl.BlockSpec(wkv_p.shape, const),
            pl.BlockSpec((3, LANES), const),
            pl.BlockSpec((3, LANES), const),
        ],
        out_specs=[
            pl.BlockSpec((H, bm, 2 * LANES), lambda i: (0, i, 0)),
            pl.BlockSpec((H, bm, 2 * LANES), lambda i: (0, i, 0)),
            pl.BlockSpec((H, bm, V_DIM), lambda i: (0, i, 0)),
        ],
        out_shape=[
            jax.ShapeDtypeStruct((H, T, 2 * LANES), BF16),
            jax.ShapeDtypeStruct((H, T, 2 * LANES), BF16),
            jax.ShapeDtypeStruct((H, T, V_DIM), BF16),
        ],
        compiler_params=_params("parallel"),
        name="mla_prep",
    )(proj, proj, proj, proj, pos, freq, sgn, mla_q_norm.reshape(1, -1), mla_kv_norm.reshape(1, -1),
      wq_p, wkv_p, gains(g_q), gains(g_k))


def _flash_kernel(q_ref, k_ref, v_ref, o_ref, m_ref, l_ref, acc_ref, *, bq):
    i = pl.program_id(2)
    q = q_ref[0]
    m_ref[...] = jnp.full(m_ref.shape, NEG_BIG, F32)
    l_ref[...] = jnp.zeros(l_ref.shape, F32)
    acc_ref[...] = jnp.zeros(acc_ref.shape, F32)

    def step(j, masked):
        start = pl.multiple_of(j * bq, bq)
        k = k_ref[0, pl.ds(start, bq), :]
        v = v_ref[0, pl.ds(start, bq), :]
        s = _dot_nt(q, k)
        if masked:
            row = lax.broadcasted_iota(jnp.int32, s.shape, 0)
            col = lax.broadcasted_iota(jnp.int32, s.shape, 1)
            s = jnp.where(col <= row, s, NEG_BIG)
        m_prev = m_ref[...]
        m_new = jnp.maximum(m_prev, jnp.max(s, axis=-1, keepdims=True))
        alpha = jnp.exp(m_prev - m_new)
        p = jnp.exp(s - m_new)
        l_ref[...] = alpha * l_ref[...] + jnp.sum(p, axis=-1, keepdims=True)
        acc_ref[...] = alpha * acc_ref[...] + _dot(p.astype(BF16), v)
        m_ref[...] = m_new

    def body(j, carry):
        step(j, False)
        return carry

    lax.fori_loop(0, i, body, 0)
    step(i, True)
    o_ref[...] = (acc_ref[...] / l_ref[...]).astype(o_ref.dtype)


def _flash_attention(q, k, v, B, S, *, bq=512):
    H = q.shape[0]
    bq = min(bq, S)
    nq = S // bq
    return pl.pallas_call(
        functools.partial(_flash_kernel, bq=bq),
        grid=(B, H, nq),
        in_specs=[
            pl.BlockSpec((1, bq, q.shape[2]), lambda b, h, i: (h, b * nq + i, 0)),
            pl.BlockSpec((1, S, k.shape[2]), lambda b, h, i: (h, b, 0)),
            pl.BlockSpec((1, S, v.shape[2]), lambda b, h, i: (h, b, 0)),
        ],
        out_specs=pl.BlockSpec((bq, V_DIM), lambda b, h, i: (b * nq + i, h)),
        out_shape=jax.ShapeDtypeStruct((B * S, H * V_DIM), BF16),
        scratch_shapes=[pltpu.VMEM((bq, 1), F32), pltpu.VMEM((bq, 1), F32),
                        pltpu.VMEM((bq, V_DIM), F32)],
        compiler_params=_params("parallel", "parallel", "arbitrary"),
        name="mla_flash",
    )(q, k, v)


def _log_sigmoid(x):
    return -(jnp.maximum(-x, 0.0) + jnp.log(1.0 + jnp.exp(-jnp.abs(x))))


def _split_dot(a, b_exact):
    hi = a.astype(BF16)
    lo = (a - hi.astype(F32)).astype(BF16)
    return _dot(hi, b_exact) + _dot(lo, b_exact)


def _mlstm_prep_kernel(xm_ref, halo_ref, cw_ref, cb_ref, wq_ref, wk_ref, wv_ref, wif_ref, wift_ref,
                       bif_ref, bift_ref, q_ref, k_ref, v_ref, xc_ref, gc_ref, gr_ref,
                       buf_ref, qkv_ref, *, bm, L):
    i = pl.program_id(1)
    H = MLSTM_HEADS
    D = xm_ref.shape[1]
    DH = D // H
    halo = halo_ref[...].astype(F32)
    buf_ref[0:MLSTM_HALO, :] = jnp.where(i > 0, halo, 0.0)
    xm_bf = xm_ref[...]
    buf_ref[MLSTM_HALO:MLSTM_HALO + bm, :] = xm_bf.astype(F32)
    off = MLSTM_HALO - (MLSTM_CONV_WIDTH - 1)
    acc = jnp.broadcast_to(cb_ref[...], (bm, D))
    for j in range(MLSTM_CONV_WIDTH):
        acc = acc + cw_ref[j:j + 1, :] * buf_ref[off + j:off + j + bm, :]
    xc = _silu(acc)
    xc_bf = xc.astype(BF16)
    xc_ref[...] = xc_bf
    for h in range(H):
        sl = slice(h * DH, (h + 1) * DH)
        qh = _dot(xc_bf[:, sl], wq_ref[h]).astype(BF16)
        k_raw = _dot(xc_bf[:, sl], wk_ref[h])
        vh = _dot(xm_bf[:, sl], wv_ref[h]).astype(BF16)
        q_ref[:, sl] = qh
        k_ref[:, sl] = (k_raw * (DH ** -0.5)).astype(BF16)
        v_ref[:, sl] = vh
        qkv_ref[:, h * DH:(h + 1) * DH] = qh
        qkv_ref[:, D + h * DH:D + (h + 1) * DH] = k_raw.astype(BF16)
        qkv_ref[:, 2 * D + h * DH:2 * D + (h + 1) * DH] = vh
    qkv = qkv_ref[...]
    pre_c = _dot(qkv, wif_ref[...]) + bif_ref[...]
    lane = lax.broadcasted_iota(jnp.int32, pre_c.shape, 1)
    is_f_c = (lane >= H) & (lane < 2 * H)
    lf_c = jnp.where(is_f_c, _log_sigmoid(pre_c), 0.0)
    r = lax.broadcasted_iota(jnp.int32, (bm, bm), 0)
    c = lax.broadcasted_iota(jnp.int32, (bm, bm), 1)
    same = (r // L) == (c // L)
    tri = jnp.where(same & (c <= r), 1.0, 0.0).astype(BF16)
    cum_c = _split_dot_left(tri, lf_c)
    gc_ref[...] = jnp.where(is_f_c, cum_c, pre_c)
    pre_r = _dot_nt(wift_ref[...], qkv) + bift_ref[...]
    row = lax.broadcasted_iota(jnp.int32, pre_r.shape, 0)
    is_f_r = row >= H
    lf_r = jnp.where(is_f_r, _log_sigmoid(pre_r), 0.0)
    trit = jnp.where(same & (r <= c), 1.0, 0.0).astype(BF16)
    cum_r = _split_dot(lf_r, trit)
    gr_ref[...] = jnp.where(is_f_r, cum_r, pre_r)


def _split_dot_left(a_exact, b):
    hi = b.astype(BF16)
    lo = (b - hi.astype(F32)).astype(BF16)
    return _dot(a_exact, hi) + _dot(a_exact, lo)


def _mlstm_prep(proj, col_xm, B, S, conv_w, conv_b, w_mq, w_mk, w_mv, w_if, b_if, *, bm=512, L):
    T = B * S
    H = MLSTM_HEADS
    D = conv_w.shape[1]
    bm = min(bm, S)
    assert bm % L == 0
    nt = S // bm
    cxm = col_xm // D
    hb = bm // MLSTM_HALO
    wif = _pad_lanes(w_if).astype(BF16)
    wift = w_if.T.astype(BF16)
    bif = _pad_lanes(b_if)[None, :]
    bift = b_if[:, None]
    const2 = lambda b_, i: (0, 0)
    const3 = lambda b_, i: (0, 0, 0)
    row_blk = lambda b_, i: (b_ * nt + i, 0)
    act = jax.ShapeDtypeStruct((T, D), BF16)
    return pl.pallas_call(
        functools.partial(_mlstm_prep_kernel, bm=bm, L=L),
        grid=(B, nt),
        in_specs=[
            pl.BlockSpec((bm, D), lambda b_, i: (b_ * nt + i, cxm)),
            pl.BlockSpec((MLSTM_HALO, D), lambda b_, i: (jnp.maximum((b_ * nt + i) * hb - 1, 0), cxm)),
            pl.BlockSpec((MLSTM_CONV_WIDTH, D), const2),
            pl.BlockSpec((1, D), const2),
            pl.BlockSpec(w_mq.shape, const3),
            pl.BlockSpec(w_mk.shape, const3),
            pl.BlockSpec(w_mv.shape, const3),
            pl.BlockSpec(wif.shape, const2),
            pl.BlockSpec(wift.shape, const2),
            pl.BlockSpec((1, LANES), const2),
            pl.BlockSpec((2 * H, 1), const2),
        ],
        out_specs=[
            pl.BlockSpec((bm, D), row_blk), pl.BlockSpec((bm, D), row_blk),
            pl.BlockSpec((bm, D), row_blk), pl.BlockSpec((bm, D), row_blk),
            pl.BlockSpec((bm, LANES), row_blk),
            pl.BlockSpec((2 * H, bm), lambda b_, i: (0, b_ * nt + i)),
        ],
        out_shape=[act, act, act, act,
                   jax.ShapeDtypeStruct((T, LANES), F32),
                   jax.ShapeDtypeStruct((2 * H, T), F32)],
        scratch_shapes=[pltpu.VMEM((MLSTM_HALO + bm, D), F32), pltpu.VMEM((bm, 3 * D), BF16)],
        compiler_params=_params("parallel", "arbitrary"),
        name="mlstm_prep",
    )(proj, proj, conv_w, conv_b.reshape(1, D), w_mq.astype(BF16), w_mk.astype(BF16),
      w_mv.astype(BF16), wif, wift, bif, bift)


def _mlstm_scan_kernel(q_ref, k_ref, v_ref, xc_ref, z_ref, gc_ref, gr_ref, gn_ref, skip_ref,
                       o_ref, c_ref, n_ref, m_ref, *, L):
    H = MLSTM_HEADS
    DH = q_ref.shape[1] // H

    @pl.when(pl.program_id(1) == 0)
    def _():
        c_ref[...] = jnp.zeros(c_ref.shape, F32)
        n_ref[...] = jnp.zeros(n_ref.shape, F32)
        m_ref[...] = jnp.zeros(m_ref.shape, F32)

    row = lax.broadcasted_iota(jnp.int32, (L, L), 0)
    col = lax.broadcasted_iota(jnp.int32, (L, L), 1)
    causal = col <= row
    for h in range(H):
        sl = slice(h * DH, (h + 1) * DH)
        q = q_ref[:, sl]
        k = k_ref[:, sl]
        v = v_ref[:, sl]
        li_c = gc_ref[:, h:h + 1]
        b_c = gc_ref[:, H + h:H + h + 1]
        li_r = gr_ref[h:h + 1, :]
        b_r = gr_ref[H + h:H + h + 1, :]
        m_prev = m_ref[h, 0:1, 0:1]
        C = c_ref[h]
        n = n_ref[h]

        dlog = jnp.where(causal, b_c - b_r + li_r, NEG_BIG)
        inter = b_c + m_prev
        m_comb = jnp.maximum(inter, jnp.max(dlog, axis=-1, keepdims=True))
        w_intra = jnp.exp(dlog - m_comb)
        w_inter = jnp.exp(inter - m_comb)
        s = _dot_nt(q, k) * w_intra
        num = w_inter * _dot(q, C.astype(BF16)) + _dot(s.astype(BF16), v)
        qf = q.astype(F32)
        den = w_inter * jnp.sum(qf * n, axis=-1, keepdims=True) + jnp.sum(s, axis=-1, keepdims=True)
        hh = num / jnp.maximum(jnp.abs(den), jnp.exp(-m_comb))

        b_last = b_c[L - 1:L, :]
        g_c = b_last - b_c + li_c
        m_new = jnp.maximum(b_last + m_prev, jnp.max(g_c, axis=0, keepdims=True))
        wg_c = jnp.exp(g_c - m_new)
        decay = jnp.exp(b_last + m_prev - m_new)
        kw = k.astype(F32) * wg_c
        c_ref[h] = decay * C + _dot(kw.T.astype(BF16), v)
        n_ref[h] = decay * n + jnp.sum(kw, axis=0, keepdims=True)
        m_ref[h] = jnp.broadcast_to(m_new, m_ref.shape[1:])

        mu = jnp.mean(hh, axis=-1, keepdims=True)
        d = hh - mu
        var = jnp.mean(d * d, axis=-1, keepdims=True)
        hn = d * lax.rsqrt(var + EPS) * gn_ref[:, sl]
        hn = hn + skip_ref[:, sl] * xc_ref[:, sl].astype(F32)
        o_ref[:, sl] = (hn * _silu(z_ref[:, sl].astype(F32))).astype(o_ref.dtype)


def _mlstm_scan(q, k, v, xc, proj, col_z, gc, gr, gn_g, skip, B, S, *, L):
    T, D = q.shape
    H = MLSTM_HEADS
    DH = D // H
    nc = S // L
    cz = col_z // D
    row_blk = lambda b_, i: (b_ * nc + i, 0)
    const2 = lambda b_, i: (0, 0)
    return pl.pallas_call(
        functools.partial(_mlstm_scan_kernel, L=L),
        grid=(B, nc),
        in_specs=[
            pl.BlockSpec((L, D), row_blk), pl.BlockSpec((L, D), row_blk),
            pl.BlockSpec((L, D), row_blk), pl.BlockSpec((L, D), row_blk),
            pl.BlockSpec((L, D), lambda b_, i: (b_ * nc + i, cz)),
            pl.BlockSpec((L, LANES), row_blk),
            pl.BlockSpec((2 * H, L), lambda b_, i: (0, b_ * nc + i)),
            pl.BlockSpec((1, D), const2),
            pl.BlockSpec((1, D), const2),
        ],
        out_specs=pl.BlockSpec((L, D), row_blk),
        out_shape=jax.ShapeDtypeStruct((T, D), BF16),
        scratch_shapes=[pltpu.VMEM((H, DH, DH), F32), pltpu.VMEM((H, 1, DH), F32),
                        pltpu.VMEM((H, 8, LANES), F32)],
        compiler_params=_params("parallel", "arbitrary"),
        name="mlstm_scan",
    )(q, k, v, xc, proj, gc, gr, gn_g.reshape(1, D), skip.reshape(1, D))


def _merge_kernel(c_ref, a_ref, m_ref, g0_ref, g1_ref, g2_ref, b0_ref, b1_ref, b2_ref,
                  wc_ref, wa_ref, wm_ref, o_ref):
    def gate(g_ref, b_ref):
        return _sigmoid(g_ref[...].astype(F32) + b_ref[...])

    y = gate(g0_ref, b0_ref) * _dot(c_ref[...], wc_ref[...])
    y = y + gate(g1_ref, b1_ref) * _dot(a_ref[...], wa_ref[...])
    y = y + gate(g2_ref, b2_ref) * _dot(m_ref[...], wm_ref[...])
    o_ref[...] = y.astype(o_ref.dtype)


def _merge(c, a, m, proj, col_gate, b_gate, wc, wa, wm, *, bm=512, bn=1024):
    T, K = c.shape
    N = wc.shape[1]
    bm, bn = min(bm, T), min(bn, N)
    nb = N // bn
    g0 = col_gate // bn
    xs = pl.BlockSpec((bm, K), lambda i, j: (i, 0))
    ws = pl.BlockSpec((K, bn), lambda i, j: (0, j))
    bg = b_gate.reshape(1, 3 * N)

    def gspec(t):
        return pl.BlockSpec((bm, bn), lambda i, j: (i, g0 + t * nb + j))

    def bspec(t):
        return pl.BlockSpec((1, bn), lambda i, j: (0, t * nb + j))

    return pl.pallas_call(
        _merge_kernel,
        grid=(T // bm, nb),
        in_specs=[xs, xs, xs, gspec(0), gspec(1), gspec(2), bspec(0), bspec(1), bspec(2), ws, ws, ws],
        out_specs=pl.BlockSpec((bm, bn), lambda i, j: (i, j)),
        out_shape=jax.ShapeDtypeStruct((T, N), BF16),
        compiler_params=_params("parallel", "arbitrary"),
        name="gated_merge",
    )(c, a, m, proj, proj, proj, bg, bg, bg, wc, wa, wm)


def _mem_kv_kernel(mem_ref, g_ref, w_ref, gk_ref, k_ref, v_ref):
    H, DH = XATTN_HEADS, XATTN_HEAD_DIM
    x = mem_ref[...]
    h = (_rms(x, x.shape[-1]) * g_ref[...]).astype(BF16)
    kv = _dot(h, w_ref[...])
    for hd in range(H):
        kh = kv[:, hd * DH:(hd + 1) * DH]
        k_ref[:, hd * DH:(hd + 1) * DH] = (_rms(kh, DH) * gk_ref[...]).astype(BF16)
    v_ref[...] = kv[:, H * DH:].astype(BF16)


def _mem_kv(mem2d, norm_mem, w_xkv, g_k, *, bm=256):
    M, D = mem2d.shape
    HD = XATTN_HEADS * XATTN_HEAD_DIM
    bm = min(bm, M)
    const = lambda i: (0, 0)
    out = jax.ShapeDtypeStruct((M, HD), BF16)
    return pl.pallas_call(
        _mem_kv_kernel,
        grid=(M // bm,),
        in_specs=[pl.BlockSpec((bm, D), lambda i: (i, 0)), pl.BlockSpec((1, D), const),
                  pl.BlockSpec((D, 2 * HD), const), pl.BlockSpec((1, XATTN_HEAD_DIM), const)],
        out_specs=[pl.BlockSpec((bm, HD), lambda i: (i, 0)), pl.BlockSpec((bm, HD), lambda i: (i, 0))],
        out_shape=[out, out],
        compiler_params=_params("parallel"),
        name="xattn_mem_kv",
    )(mem2d, norm_mem.reshape(1, D), w_xkv.astype(BF16), g_k.reshape(1, -1))


def _xattn_kernel(x_ref, g_ref, wq_ref, gq_ref, k_ref, v_ref, wo_ref, o_ref, att_ref, *, scale):
    H, DH = XATTN_HEADS, XATTN_HEAD_DIM
    x = x_ref[...]
    h = (_rms(x, x.shape[-1]) * g_ref[...]).astype(BF16)
    q = _dot(h, wq_ref[...])
    for hd in range(H):
        sl = slice(hd * DH, (hd + 1) * DH)
        qh = (_rms(q[:, sl], DH) * gq_ref[...] * scale).astype(BF16)
        s = _dot_nt(qh, k_ref[:, sl])
        e = jnp.exp(s - jnp.max(s, axis=-1, keepdims=True))
        o = _dot(e.astype(BF16), v_ref[:, sl]) / jnp.sum(e, axis=-1, keepdims=True)
        att_ref[:, sl] = o.astype(BF16)
    o_ref[...] = x + _dot(att_ref[...], wo_ref[...])


def _xattn(x, norm_x, w_xq, g_q, k, v, w_xo, B, S, *, bm=512):
    T, D = x.shape
    HD = XATTN_HEADS * XATTN_HEAD_DIM
    M = k.shape[0] // B
    bm = min(bm, S)
    nt = S // bm
    const = lambda b_, i: (0, 0)
    return pl.pallas_call(
        functools.partial(_xattn_kernel, scale=float(XATTN_HEAD_DIM ** -0.5)),
        grid=(B, nt),
        in_specs=[
            pl.BlockSpec((bm, D), lambda b_, i: (b_ * nt + i, 0)),
            pl.BlockSpec((1, D), const),
            pl.BlockSpec((D, HD), const),
            pl.BlockSpec((1, XATTN_HEAD_DIM), const),
            pl.BlockSpec((M, HD), lambda b_, i: (b_, 0)),
            pl.BlockSpec((M, HD), lambda b_, i: (b_, 0)),
            pl.BlockSpec((HD, D), const),
        ],
        out_specs=pl.BlockSpec((bm, D), lambda b_, i: (b_ * nt + i, 0)),
        out_shape=jax.ShapeDtypeStruct((T, D), F32),
        scratch_shapes=[pltpu.VMEM((bm, HD), BF16)],
        compiler_params=_params("parallel", "arbitrary"),
        name="xattn",
    )(x, norm_x.reshape(1, D), w_xq.astype(BF16), g_q.reshape(1, -1), k, v, w_xo.astype(BF16))


def _layout_w_in(w_in, C, q_lora, kv_lora, D_ml, n_gate):
    o = np.cumsum([0, 2 * C, q_lora, kv_lora, ROPE_DIM, D_ml, D_ml, n_gate])
    conv, cq, ckv, kr, xm, z, gate = (w_in[:, o[t]:o[t + 1]] for t in range(7))
    segs = [gate, conv, xm, z, cq, ckv, _pad_lanes(kr), _pad_lanes(_swap_halves(kr))]
    w = jnp.concatenate(segs, axis=1)
    offs = np.cumsum([0] + [s.shape[1] for s in segs])
    pad = (-w.shape[1]) % 1024
    w = jnp.pad(w, ((0, 0), (0, pad)))
    cols = dict(gate=int(offs[0]), a=int(offs[1]), g=int(offs[1]) + C, xm=int(offs[2]), z=int(offs[3]),
                cq=int(offs[4]), ckv=int(offs[5]), kr=int(offs[6]), krs=int(offs[7]))
    return w.astype(BF16), cols


def kernel(x, mem, positions, norm_mix, w_in, b_gate, conv_dw, conv_dw_b, conv_ln_g, conv_ln_b, w_conv_out, mla_q_norm, mla_kv_norm, w_q_up, w_kv_up, mla_g_q, mla_g_k, w_mla_out, mlstm_conv_w, mlstm_conv_b, w_mq, w_mk, w_mv, w_if, b_if, mlstm_gn_g, mlstm_skip, w_mlstm_out, w_mix_out, norm_x, norm_mem, w_xq, w_xkv, xattn_g_q, xattn_g_k, w_xo, norm_ffn, w_ffn_in, w_ffn_out):
    B, S, D = x.shape
    T = B * S
    depth = w_in.shape[0]
    C = conv_dw.shape[2]
    D_ml = mlstm_conv_w.shape[2]
    L = min(MLSTM_CHUNK, S)
    xt = x.reshape(T, D)
    pos = positions.reshape(T, 1)
    mem2d = mem.reshape(-1, D)
    for l in range(depth):
        w_in_p, cols = _layout_w_in(w_in[l], C, w_q_up.shape[1], w_kv_up.shape[1], D_ml, b_gate.shape[1])
        proj = _matmul(xt, w_in_p, bm=512, bn=1024, gain=norm_mix[l], name="in_proj")
        c = _conv_branch(proj, cols["a"], cols["g"], C, B, S, conv_dw[l], conv_dw_b[l],
                         conv_ln_g[l], conv_ln_b[l])
        q, k, v = _mla_prep(proj, (cols["cq"], cols["ckv"], cols["kr"], cols["krs"]), pos,
                            mla_q_norm[l], mla_kv_norm[l], w_q_up[l], w_kv_up[l], mla_g_q[l], mla_g_k[l])
        a = _flash_attention(q, k, v, B, S)
        mq, mk, mv, xc, gc, gr = _mlstm_prep(proj, cols["xm"], B, S, mlstm_conv_w[l], mlstm_conv_b[l],
                                             w_mq[l], w_mk[l], w_mv[l], w_if[l], b_if[l], L=L)
        m = _mlstm_scan(mq, mk, mv, xc, proj, cols["z"], gc, gr, mlstm_gn_g[l], mlstm_skip[l], B, S, L=L)
        merged = _merge(c, a, m, proj, cols["gate"], b_gate[l], w_conv_out[l].astype(BF16),
                        w_mla_out[l].astype(BF16), w_mlstm_out[l].astype(BF16))
        xt = _matmul(merged, w_mix_out[l].astype(BF16), bm=512, bn=1024, residual=xt, out_dtype=F32,
                     name="mix_out")
        xk, xv = _mem_kv(mem2d, norm_mem[l], w_xkv[l], xattn_g_k[l])
        xt = _xattn(xt, norm_x[l], w_xq[l], xattn_g_q[l], xk, xv, w_xo[l], B, S)
        y = _matmul(xt, w_ffn_in[l].astype(BF16), bm=512, bn=512, gain=norm_ffn[l], swiglu=True,
                    name="ffn_in")
        xt = _matmul(y, w_ffn_out[l].astype(BF16), bm=512, bn=512, residual=xt, out_dtype=F32,
                     name="ffn_out")
    return xt.reshape(B, S, D)
```

```python
import functools

import jax
import jax.numpy as jnp
import numpy as np
from jax import lax
from jax.experimental import pallas as pl
from jax.experimental.pallas import tpu as pltpu

F32 = jnp.float32
BF16 = jnp.bfloat16

EPS = 1e-6
CONV_WIDTH = 31
MLA_HEADS = 8
NOPE_DIM = 128
ROPE_DIM = 64
V_DIM = 128
ROPE_BASE = 10000.0
MLSTM_HEADS = 4
MLSTM_CONV_WIDTH = 4
XATTN_HEADS = 4
XATTN_HEAD_DIM = 128
LANES = 128
SUBLANES = 8
NEG_BIG = -1e30
VMEM_LIMIT = 56 * 1024 * 1024

MLSTM_CHUNK = 256
CONV_HALO = 32
MLSTM_HALO = 16


def _params(*sem):
    return pltpu.CompilerParams(dimension_semantics=sem, vmem_limit_bytes=VMEM_LIMIT)


def _sigmoid(x):
    return 1.0 / (1.0 + jnp.exp(-x))


def _silu(x):
    return x * _sigmoid(x)


def _dot(a, b):
    return jnp.dot(a, b, preferred_element_type=F32)


def _dot_nt(a, b):
    return lax.dot_general(a, b, (((1,), (1,)), ((), ())), preferred_element_type=F32)


def _rms(x, n):
    ms = jnp.sum(x * x, axis=-1, keepdims=True) * (1.0 / n)
    return x * lax.rsqrt(ms + EPS)


def _mm_kernel(*refs, norm, swiglu, has_res):
    it = iter(refs)
    x_ref = next(it)
    g_ref = next(it) if norm else None
    w_ref = next(it)
    w2_ref = next(it) if swiglu else None
    r_ref = next(it) if has_res else None
    o_ref = next(it)
    h_ref = next(it) if norm else None
    if norm:
        @pl.when(pl.program_id(1) == 0)
        def _():
            xf = x_ref[...].astype(F32)
            h_ref[...] = (_rms(xf, xf.shape[-1]) * g_ref[...]).astype(BF16)
        h = h_ref[...]
    else:
        h = x_ref[...]
    acc = _dot(h, w_ref[...])
    if swiglu:
        acc = _silu(acc) * _dot(h, w2_ref[...])
    if has_res:
        acc = r_ref[...] + acc
    o_ref[...] = acc.astype(o_ref.dtype)


def _matmul(x, w, *, bm, bn, gain=None, swiglu=False, residual=None, out_dtype=BF16,
            x_cols=None, name="matmul"):
    M = x.shape[0]
    c0, K = (0, x.shape[1]) if x_cols is None else x_cols
    assert c0 % K == 0 and w.shape[0] == K
    N = w.shape[1] // 2 if swiglu else w.shape[1]
    bm, bn = min(bm, M), min(bn, N)
    assert M % bm == 0 and N % bn == 0
    xcb = c0 // K
    norm = gain is not None
    in_specs = [pl.BlockSpec((bm, K), lambda i, j: (i, xcb))]
    args = [x]
    if norm:
        in_specs.append(pl.BlockSpec((1, K), lambda i, j: (0, 0)))
        args.append(gain.reshape(1, K).astype(F32))
    in_specs.append(pl.BlockSpec((K, bn), lambda i, j: (0, j)))
    args.append(w)
    if swiglu:
        nb = N // bn
        in_specs.append(pl.BlockSpec((K, bn), lambda i, j: (0, j + nb)))
        args.append(w)
    if residual is not None:
        in_specs.append(pl.BlockSpec((bm, bn), lambda i, j: (i, j)))
        args.append(residual)
    scratch = [pltpu.VMEM((bm, K), BF16)] if norm else []
    return pl.pallas_call(
        functools.partial(_mm_kernel, norm=norm, swiglu=swiglu, has_res=residual is not None),
        grid=(M // bm, N // bn),
        in_specs=in_specs,
        out_specs=pl.BlockSpec((bm, bn), lambda i, j: (i, j)),
        out_shape=jax.ShapeDtypeStruct((M, N), out_dtype),
        scratch_shapes=scratch,
        compiler_params=_params("parallel", "arbitrary"),
        name=name,
    )(*args)


def _conv_kernel(a_ref, g_ref, w_ref, b_ref, lg_ref, lb_ref, o_ref, u_ref, sh_ref, *, bt, rt):
    i = pl.program_id(1)

    @pl.when(i == 0)
    def _():
        u_ref[0:CONV_HALO, :] = jnp.zeros((CONV_HALO, u_ref.shape[1]), F32)

    @pl.when(i > 0)
    def _():
        u_ref[0:CONV_HALO, :] = u_ref[bt:bt + CONV_HALO, :]

    a = a_ref[...].astype(F32)
    g = g_ref[...].astype(F32)
    u_ref[CONV_HALO:CONV_HALO + bt, :] = a * _sigmoid(g)
    n_sh = CONV_HALO + bt - SUBLANES
    for r in range(1, SUBLANES):
        sh_ref[r - 1, SUBLANES:, :] = u_ref[SUBLANES - r:SUBLANES - r + n_sh, :]
    bias = b_ref[...]
    for r0 in range(0, bt, rt):
        acc = jnp.broadcast_to(bias, (rt, bias.shape[1]))
        for j in range(CONV_WIDTH):
            shift = CONV_WIDTH - 1 - j
            r, start = shift % SUBLANES, r0 + CONV_HALO - (shift // SUBLANES) * SUBLANES
            win = u_ref[start:start + rt, :] if r == 0 else sh_ref[r - 1, start:start + rt, :]
            acc = acc + w_ref[j:j + 1, :] * win
        mu = jnp.mean(acc, axis=-1, keepdims=True)
        d = acc - mu
        var = jnp.mean(d * d, axis=-1, keepdims=True)
        y = d * lax.rsqrt(var + EPS) * lg_ref[...] + lb_ref[...]
        o_ref[r0:r0 + rt, :] = _silu(y).astype(o_ref.dtype)


def _conv_branch(proj, col_a, col_g, C, B, S, w, b, ln_g, ln_b, *, bt=256, rt=32):
    bt = min(bt, S)
    rt = min(rt, bt)
    nt = S // bt
    ca, cg = col_a // C, col_g // C
    return pl.pallas_call(
        functools.partial(_conv_kernel, bt=bt, rt=rt),
        grid=(B, nt),
        in_specs=[
            pl.BlockSpec((bt, C), lambda b_, i: (b_ * nt + i, ca)),
            pl.BlockSpec((bt, C), lambda b_, i: (b_ * nt + i, cg)),
            pl.BlockSpec((CONV_WIDTH, C), lambda b_, i: (0, 0)),
            pl.BlockSpec((1, C), lambda b_, i: (0, 0)),
            pl.BlockSpec((1, C), lambda b_, i: (0, 0)),
            pl.BlockSpec((1, C), lambda b_, i: (0, 0)),
        ],
        out_specs=pl.BlockSpec((bt, C), lambda b_, i: (b_ * nt + i, 0)),
        out_shape=jax.ShapeDtypeStruct((B * S, C), BF16),
        scratch_shapes=[pltpu.VMEM((CONV_HALO + bt, C), F32),
                        pltpu.VMEM((SUBLANES - 1, CONV_HALO + bt, C), F32)],
        compiler_params=_params("parallel", "arbitrary"),
        name="conformer_conv",
    )(proj, proj, w, b.reshape(1, C), ln_g.reshape(1, C), ln_b.reshape(1, C))


def _mla_prep_kernel(cq_ref, ckv_ref, kr_ref, krs_ref, pos_ref, freq_ref, sgn_ref,
                     qn_ref, kvn_ref, wq_ref, wkv_ref, gq_ref, gk_ref,
                     q_ref, k_ref, v_ref, *, scale):
    H = MLA_HEADS
    pos = pos_ref[...].astype(F32)
    ang = pos * freq_ref[...]
    cs1 = jnp.cos(ang)
    cs2 = jnp.sin(ang) * sgn_ref[...]

    cq = cq_ref[...].astype(F32)
    hq = (_rms(cq, cq.shape[-1]) * qn_ref[...]).astype(BF16)
    qall = _dot(hq, wq_ref[...])
    ckv = ckv_ref[...].astype(F32)
    hkv = (_rms(ckv, ckv.shape[-1]) * kvn_ref[...]).astype(BF16)
    kvall = _dot(hkv, wkv_ref[...])

    g_nope_q, g_rot_q, g_rots_q = gq_ref[0:1, :], gq_ref[1:2, :], gq_ref[2:3, :]
    g_nope_k, g_rot_k, g_rots_k = gk_ref[0:1, :], gk_ref[1:2, :], gk_ref[2:3, :]

    kr = kr_ref[...].astype(F32)
    krs = krs_ref[...].astype(F32)
    rs = lax.rsqrt(jnp.sum(kr * kr, axis=-1, keepdims=True) * (1.0 / ROPE_DIM) + EPS)
    k_rot = ((kr * rs * g_rot_k) * cs1 + (krs * rs * g_rots_k) * cs2).astype(BF16)

    for h in range(H):
        qn = qall[:, h * LANES:(h + 1) * LANES]
        q_ref[h, :, 0:LANES] = (_rms(qn, NOPE_DIM) * g_nope_q * scale).astype(BF16)
        qr = qall[:, (H + h) * LANES:(H + h + 1) * LANES]
        qs = qall[:, (2 * H + h) * LANES:(2 * H + h + 1) * LANES]
        rq = lax.rsqrt(jnp.sum(qr * qr, axis=-1, keepdims=True) * (1.0 / ROPE_DIM) + EPS)
        q_rot = (qr * rq * g_rot_q) * cs1 + (qs * rq * g_rots_q) * cs2
        q_ref[h, :, LANES:2 * LANES] = (q_rot * scale).astype(BF16)
        kn = kvall[:, h * LANES:(h + 1) * LANES]
        k_ref[h, :, 0:LANES] = (_rms(kn, NOPE_DIM) * g_nope_k).astype(BF16)
        k_ref[h, :, LANES:2 * LANES] = k_rot
        v_ref[h, :, 0:V_DIM] = kvall[:, (H + h) * LANES:(H + h + 1) * LANES].astype(BF16)
        v_ref[h, :, V_DIM:V_DIM + LANES] = jnp.ones((kvall.shape[0], LANES), BF16)


def _pad_lanes(v, n=LANES):
    return jnp.pad(v, [(0, 0)] * (v.ndim - 1) + [(0, n - v.shape[-1])])


def _swap_halves(v):
    half = v.shape[-1] // 2
    return jnp.concatenate([v[..., half:], v[..., :half]], axis=-1)


def _mla_prep(proj, cols, pos, mla_q_norm, mla_kv_norm, w_q_up, w_kv_up, g_q, g_k, *, bm=512):
    T = proj.shape[0]
    bm = min(bm, T)
    H = MLA_HEADS
    q_lora, kv_lora = w_q_up.shape[0], w_kv_up.shape[0]
    c_cq, c_ckv, c_kr, c_krs = cols
    half = ROPE_DIM // 2
    inv_freq = ROPE_BASE ** (-jnp.arange(half, dtype=F32) / half)
    freq = _pad_lanes(jnp.concatenate([inv_freq, inv_freq]))[None, :]
    sgn = _pad_lanes(jnp.concatenate([-jnp.ones((half,), F32), jnp.ones((half,), F32)]))[None, :]
    wq = w_q_up.reshape(q_lora, H, NOPE_DIM + ROPE_DIM)
    wq_rot = wq[:, :, NOPE_DIM:]
    wq_p = jnp.concatenate([
        wq[:, :, :NOPE_DIM].reshape(q_lora, H * NOPE_DIM),
        _pad_lanes(wq_rot).reshape(q_lora, H * LANES),
        _pad_lanes(_swap_halves(wq_rot)).reshape(q_lora, H * LANES)], axis=1).astype(BF16)
    wkv = w_kv_up.reshape(kv_lora, H, NOPE_DIM + V_DIM)
    wkv_p = jnp.concatenate([wkv[:, :, :NOPE_DIM].reshape(kv_lora, H * NOPE_DIM),
                             wkv[:, :, NOPE_DIM:].reshape(kv_lora, H * V_DIM)], axis=1).astype(BF16)

    def gains(g):
        return jnp.stack([g[:NOPE_DIM], _pad_lanes(g[NOPE_DIM:]), _pad_lanes(_swap_halves(g[NOPE_DIM:]))])

    scale = float((NOPE_DIM + ROPE_DIM) ** -0.5 * np.log2(np.e))
    const = lambda i: (0, 0)
    return pl.pallas_call(
        functools.partial(_mla_prep_kernel, scale=scale),
        grid=(T // bm,),
        in_specs=[
            pl.BlockSpec((bm, q_lora), lambda i: (i, c_cq // q_lora)),
            pl.BlockSpec((bm, kv_lora), lambda i: (i, c_ckv // kv_lora)),
            pl.BlockSpec((bm, LANES), lambda i: (i, c_kr // LANES)),
            pl.BlockSpec((bm, LANES), lambda i: (i, c_krs // LANES)),
            pl.BlockSpec((bm, 1), lambda i: (i, 0)),
            pl.BlockSpec((1, LANES), const),
            pl.BlockSpec((1, LANES), const),
            pl.BlockSpec((1, q_lora), const),
            pl.BlockSpec((1, kv_lora), const),
            pl.BlockSpec(wq_p.shape, const),
            pl.BlockSpec(wkv_p.shape, const),
            pl.BlockSpec((3, LANES), const),
            pl.BlockSpec((3, LANES), const),
        ],
        out_specs=[
            pl.BlockSpec((H, bm, 2 * LANES), lambda i: (0, i, 0)),
            pl.BlockSpec((H, bm, 2 * LANES), lambda i: (0, i, 0)),
            pl.BlockSpec((H, bm, V_DIM + LANES), lambda i: (0, i, 0)),
        ],
        out_shape=[
            jax.ShapeDtypeStruct((H, T, 2 * LANES), BF16),
            jax.ShapeDtypeStruct((H, T, 2 * LANES), BF16),
            jax.ShapeDtypeStruct((H, T, V_DIM + LANES), BF16),
        ],
        compiler_params=_params("parallel"),
        name="mla_prep",
    )(proj, proj, proj, proj, pos, freq, sgn, mla_q_norm.reshape(1, -1), mla_kv_norm.reshape(1, -1),
      wq_p, wkv_p, gains(g_q), gains(g_k))


def _flash_kernel(q_ref, k_ref, v_ref, o_ref, m_ref, acc_ref, *, bh, nh):
    i = pl.program_id(2)
    m_ref[...] = jnp.full(m_ref.shape, NEG_BIG, F32)
    acc_ref[...] = jnp.zeros(acc_ref.shape, F32)
    nc = bh // LANES

    def chain(a, j, masked):
        rows = slice(a * bh, (a + 1) * bh)
        start = pl.multiple_of(j * bh, bh)
        k = k_ref[0, pl.ds(start, bh), :]
        v = v_ref[0, pl.ds(start, bh), :]
        s = _dot_nt(q_ref[0, rows, :], k)
        if masked:
            row = lax.broadcasted_iota(jnp.int32, s.shape, 0)
            col = lax.broadcasted_iota(jnp.int32, s.shape, 1)
            s = jnp.where(col <= row, s, NEG_BIG)
        chunks = [s[:, c * LANES:(c + 1) * LANES] for c in range(nc)]
        mx = functools.reduce(jnp.maximum, chunks)
        m_prev = m_ref[rows, :]
        m_new = jnp.maximum(m_prev, jnp.max(mx, axis=-1, keepdims=True))
        alpha = jnp.exp2(m_prev - m_new)
        p = jnp.concatenate([jnp.exp2(c - m_new) for c in chunks], axis=1).astype(BF16)
        alpha2 = jnp.concatenate([alpha] * (acc_ref.shape[1] // LANES), axis=1)
        acc_ref[rows, :] = alpha2 * acc_ref[rows, :] + _dot(p, v)
        m_ref[rows, :] = m_new

    def body(j, carry):
        for a in range(nh):
            chain(a, j, False)
        return carry

    lax.fori_loop(0, i * nh, body, 0)
    for a in range(nh):
        for c in range(a + 1):
            chain(a, i * nh + c, c == a)
    o_ref[...] = (acc_ref[:, 0:V_DIM] / acc_ref[:, V_DIM:2 * V_DIM]).astype(o_ref.dtype)


def _flash_attention(q, k, v, B, S, *, bh=512, nh=2):
    H = q.shape[0]
    bh = min(bh, S // nh)
    bq = bh * nh
    nq = S // bq
    return pl.pallas_call(
        functools.partial(_flash_kernel, bh=bh, nh=nh),
        grid=(B, H, nq),
        in_specs=[
            pl.BlockSpec((1, bq, q.shape[2]), lambda b, h, i: (h, b * nq + i, 0)),
            pl.BlockSpec((1, S, k.shape[2]), lambda b, h, i: (h, b, 0)),
            pl.BlockSpec((1, S, v.shape[2]), lambda b, h, i: (h, b, 0)),
        ],
        out_specs=pl.BlockSpec((bq, V_DIM), lambda b, h, i: (b * nq + i, h)),
        out_shape=jax.ShapeDtypeStruct((B * S, H * V_DIM), BF16),
        scratch_shapes=[pltpu.VMEM((bq, LANES), F32), pltpu.VMEM((bq, v.shape[2]), F32)],
        compiler_params=_params("parallel", "parallel", "arbitrary"),
        name="mla_flash",
    )(q, k, v)


def _log_sigmoid(x):
    return -(jnp.maximum(-x, 0.0) + jnp.log(1.0 + jnp.exp(-jnp.abs(x))))


def _split_dot(a, b_exact):
    hi = a.astype(BF16)
    lo = (a - hi.astype(F32)).astype(BF16)
    return _dot(hi, b_exact) + _dot(lo, b_exact)


def _mlstm_prep_kernel(xm_ref, halo_ref, cw_ref, cb_ref, wq_ref, wk_ref, wv_ref, wif_ref, wift_ref,
                       bif_ref, bift_ref, q_ref, k_ref, v_ref, xc_ref, gc_ref, gr_ref,
                       buf_ref, qkv_ref, *, bm, L):
    i = pl.program_id(1)
    H = MLSTM_HEADS
    D = xm_ref.shape[1]
    DH = D // H
    halo = halo_ref[...].astype(F32)
    buf_ref[0:MLSTM_HALO, :] = jnp.where(i > 0, halo, 0.0)
    xm_bf = xm_ref[...]
    buf_ref[MLSTM_HALO:MLSTM_HALO + bm, :] = xm_bf.astype(F32)
    off = MLSTM_HALO - (MLSTM_CONV_WIDTH - 1)
    acc = jnp.broadcast_to(cb_ref[...], (bm, D))
    for j in range(MLSTM_CONV_WIDTH):
        acc = acc + cw_ref[j:j + 1, :] * buf_ref[off + j:off + j + bm, :]
    xc = _silu(acc)
    xc_bf = xc.astype(BF16)
    xc_ref[...] = xc_bf
    for h in range(H):
        sl = slice(h * DH, (h + 1) * DH)
        qh = _dot(xc_bf[:, sl], wq_ref[h]).astype(BF16)
        k_raw = _dot(xc_bf[:, sl], wk_ref[h])
        vh = _dot(xm_bf[:, sl], wv_ref[h]).astype(BF16)
        q_ref[:, sl] = qh
        k_ref[:, sl] = (k_raw * (DH ** -0.5)).astype(BF16)
        v_ref[:, sl] = vh
        qkv_ref[:, h * DH:(h + 1) * DH] = qh
        qkv_ref[:, D + h * DH:D + (h + 1) * DH] = k_raw.astype(BF16)
        qkv_ref[:, 2 * D + h * DH:2 * D + (h + 1) * DH] = vh
    qkv = qkv_ref[...]
    pre_c = _dot(qkv, wif_ref[...]) + bif_ref[...]
    lane = lax.broadcasted_iota(jnp.int32, pre_c.shape, 1)
    is_f_c = (lane >= H) & (lane < 2 * H)
    lf_c = jnp.where(is_f_c, _log_sigmoid(pre_c), 0.0)
    r = lax.broadcasted_iota(jnp.int32, (bm, bm), 0)
    c = lax.broadcasted_iota(jnp.int32, (bm, bm), 1)
    same = (r // L) == (c // L)
    tri = jnp.where(same & (c <= r), 1.0, 0.0).astype(BF16)
    cum_c = _split_dot_left(tri, lf_c)
    gc_ref[...] = jnp.where(is_f_c, cum_c, pre_c)
    pre_r = _dot_nt(wift_ref[...], qkv) + bift_ref[...]
    row = lax.broadcasted_iota(jnp.int32, pre_r.shape, 0)
    is_f_r = row >= H
    lf_r = jnp.where(is_f_r, _log_sigmoid(pre_r), 0.0)
    trit = jnp.where(same & (r <= c), 1.0, 0.0).astype(BF16)
    cum_r = _split_dot(lf_r, trit)
    gr_ref[...] = jnp.where(is_f_r, cum_r, pre_r)


def _split_dot_left(a_exact, b):
    hi = b.astype(BF16)
    lo = (b - hi.astype(F32)).astype(BF16)
    return _dot(a_exact, hi) + _dot(a_exact, lo)


def _mlstm_prep(proj, col_xm, B, S, conv_w, conv_b, w_mq, w_mk, w_mv, w_if, b_if, *, bm=512, L):
    T = B * S
    H = MLSTM_HEADS
    D = conv_w.shape[1]
    bm = min(bm, S)
    assert bm % L == 0
    nt = S // bm
    cxm = col_xm // D
    hb = bm // MLSTM_HALO
    wif = _pad_lanes(w_if).astype(BF16)
    wift = w_if.T.astype(BF16)
    bif = _pad_lanes(b_if)[None, :]
    bift = b_if[:, None]
    const2 = lambda b_, i: (0, 0)
    const3 = lambda b_, i: (0, 0, 0)
    row_blk = lambda b_, i: (b_ * nt + i, 0)
    act = jax.ShapeDtypeStruct((T, D), BF16)
    return pl.pallas_call(
        functools.partial(_mlstm_prep_kernel, bm=bm, L=L),
        grid=(B, nt),
        in_specs=[
            pl.BlockSpec((bm, D), lambda b_, i: (b_ * nt + i, cxm)),
            pl.BlockSpec((MLSTM_HALO, D), lambda b_, i: (jnp.maximum((b_ * nt + i) * hb - 1, 0), cxm)),
            pl.BlockSpec((MLSTM_CONV_WIDTH, D), const2),
            pl.BlockSpec((1, D), const2),
            pl.BlockSpec(w_mq.shape, const3),
            pl.BlockSpec(w_mk.shape, const3),
            pl.BlockSpec(w_mv.shape, const3),
            pl.BlockSpec(wif.shape, const2),
            pl.BlockSpec(wift.shape, const2),
            pl.BlockSpec((1, LANES), const2),
            pl.BlockSpec((2 * H, 1), const2),
        ],
        out_specs=[
            pl.BlockSpec((bm, D), row_blk), pl.BlockSpec((bm, D), row_blk),
            pl.BlockSpec((bm, D), row_blk), pl.BlockSpec((bm, D), row_blk),
            pl.BlockSpec((bm, LANES), row_blk),
            pl.BlockSpec((2 * H, bm), lambda b_, i: (0, b_ * nt + i)),
        ],
        out_shape=[act, act, act, act,
                   jax.ShapeDtypeStruct((T, LANES), F32),
                   jax.ShapeDtypeStruct((2 * H, T), F32)],
        scratch_shapes=[pltpu.VMEM((MLSTM_HALO + bm, D), F32), pltpu.VMEM((bm, 3 * D), BF16)],
        compiler_params=_params("parallel", "arbitrary"),
        name="mlstm_prep",
    )(proj, proj, conv_w, conv_b.reshape(1, D), w_mq.astype(BF16), w_mk.astype(BF16),
      w_mv.astype(BF16), wif, wift, bif, bift)


def _mlstm_scan_kernel(q_ref, k_ref, v_ref, xc_ref, z_ref, gc_ref, gr_ref, gn_ref, skip_ref,
                       o_ref, c_ref, n_ref, m_ref, *, L):
    H = MLSTM_HEADS
    DH = q_ref.shape[1] // H

    @pl.when(pl.program_id(1) == 0)
    def _():
        c_ref[...] = jnp.zeros(c_ref.shape, F32)
        n_ref[...] = jnp.zeros(n_ref.shape, F32)
        m_ref[...] = jnp.zeros(m_ref.shape, F32)

    row = lax.broadcasted_iota(jnp.int32, (L, L), 0)
    col = lax.broadcasted_iota(jnp.int32, (L, L), 1)
    causal = col <= row
    for h in range(H):
        sl = slice(h * DH, (h + 1) * DH)
        q = q_ref[:, sl]
        k = k_ref[:, sl]
        v = v_ref[:, sl]
        li_c = gc_ref[:, h:h + 1]
        b_c = gc_ref[:, H + h:H + h + 1]
        li_r = gr_ref[h:h + 1, :]
        b_r = gr_ref[H + h:H + h + 1, :]
        m_prev = m_ref[h, 0:1, 0:1]
        C = c_ref[h]
        n = n_ref[h]

        dlog = jnp.where(causal, b_c - b_r + li_r, NEG_BIG)
        inter = b_c + m_prev
        m_comb = jnp.maximum(inter, jnp.max(dlog, axis=-1, keepdims=True))
        w_intra = jnp.exp(dlog - m_comb)
        w_inter = jnp.exp(inter - m_comb)
        s = _dot_nt(q, k) * w_intra
        num = w_inter * _dot(q, C.astype(BF16)) + _dot(s.astype(BF16), v)
        qf = q.astype(F32)
        den = w_inter * jnp.sum(qf * n, axis=-1, keepdims=True) + jnp.sum(s, axis=-1, keepdims=True)
        hh = num / jnp.maximum(jnp.abs(den), jnp.exp(-m_comb))

        b_last = b_c[L - 1:L, :]
        g_c = b_last - b_c + li_c
        m_new = jnp.maximum(b_last + m_prev, jnp.max(g_c, axis=0, keepdims=True))
        wg_c = jnp.exp(g_c - m_new)
        decay = jnp.exp(b_last + m_prev - m_new)
        kw = k.astype(F32) * wg_c
        c_ref[h] = decay * C + _dot(kw.T.astype(BF16), v)
        n_ref[h] = decay * n + jnp.sum(kw, axis=0, keepdims=True)
        m_ref[h] = jnp.broadcast_to(m_new, m_ref.shape[1:])

        mu = jnp.mean(hh, axis=-1, keepdims=True)
        d = hh - mu
        var = jnp.mean(d * d, axis=-1, keepdims=True)
        hn = d * lax.rsqrt(var + EPS) * gn_ref[:, sl]
        hn = hn + skip_ref[:, sl] * xc_ref[:, sl].astype(F32)
        o_ref[:, sl] = (hn * _silu(z_ref[:, sl].astype(F32))).astype(o_ref.dtype)


def _mlstm_scan(q, k, v, xc, proj, col_z, gc, gr, gn_g, skip, B, S, *, L):
    T, D = q.shape
    H = MLSTM_HEADS
    DH = D // H
    nc = S // L
    cz = col_z // D
    row_blk = lambda b_, i: (b_ * nc + i, 0)
    const2 = lambda b_, i: (0, 0)
    return pl.pallas_call(
        functools.partial(_mlstm_scan_kernel, L=L),
        grid=(B, nc),
        in_specs=[
            pl.BlockSpec((L, D), row_blk), pl.BlockSpec((L, D), row_blk),
            pl.BlockSpec((L, D), row_blk), pl.BlockSpec((L, D), row_blk),
            pl.BlockSpec((L, D), lambda b_, i: (b_ * nc + i, cz)),
            pl.BlockSpec((L, LANES), row_blk),
            pl.BlockSpec((2 * H, L), lambda b_, i: (0, b_ * nc + i)),
            pl.BlockSpec((1, D), const2),
            pl.BlockSpec((1, D), const2),
        ],
        out_specs=pl.BlockSpec((L, D), row_blk),
        out_shape=jax.ShapeDtypeStruct((T, D), BF16),
        scratch_shapes=[pltpu.VMEM((H, DH, DH), F32), pltpu.VMEM((H, 1, DH), F32),
                        pltpu.VMEM((H, 8, LANES), F32)],
        compiler_params=_params("parallel", "arbitrary"),
        name="mlstm_scan",
    )(q, k, v, xc, proj, gc, gr, gn_g.reshape(1, D), skip.reshape(1, D))


def _merge_kernel(c_ref, a_ref, m_ref, g0_ref, g1_ref, g2_ref, b0_ref, b1_ref, b2_ref,
                  wc_ref, wa_ref, wm_ref, o_ref):
    def gate(g_ref, b_ref):
        return _sigmoid(g_ref[...].astype(F32) + b_ref[...])

    y = gate(g0_ref, b0_ref) * _dot(c_ref[...], wc_ref[...])
    y = y + gate(g1_ref, b1_ref) * _dot(a_ref[...], wa_ref[...])
    y = y + gate(g2_ref, b2_ref) * _dot(m_ref[...], wm_ref[...])
    o_ref[...] = y.astype(o_ref.dtype)


def _merge(c, a, m, proj, col_gate, b_gate, wc, wa, wm, *, bm=512, bn=1024):
    T, K = c.shape
    N = wc.shape[1]
    bm, bn = min(bm, T), min(bn, N)
    nb = N // bn
    g0 = col_gate // bn
    xs = pl.BlockSpec((bm, K), lambda i, j: (i, 0))
    ws = pl.BlockSpec((K, bn), lambda i, j: (0, j))
    bg = b_gate.reshape(1, 3 * N)

    def gspec(t):
        return pl.BlockSpec((bm, bn), lambda i, j: (i, g0 + t * nb + j))

    def bspec(t):
        return pl.BlockSpec((1, bn), lambda i, j: (0, t * nb + j))

    return pl.pallas_call(
        _merge_kernel,
        grid=(T // bm, nb),
        in_specs=[xs, xs, xs, gspec(0), gspec(1), gspec(2), bspec(0), bspec(1), bspec(2), ws, ws, ws],
        out_specs=pl.BlockSpec((bm, bn), lambda i, j: (i, j)),
        out_shape=jax.ShapeDtypeStruct((T, N), BF16),
        compiler_params=_params("parallel", "arbitrary"),
        name="gated_merge",
    )(c, a, m, proj, proj, proj, bg, bg, bg, wc, wa, wm)


def _mem_kv_kernel(mem_ref, g_ref, w_ref, gk_ref, k_ref, v_ref):
    H, DH = XATTN_HEADS, XATTN_HEAD_DIM
    x = mem_ref[...]
    h = (_rms(x, x.shape[-1]) * g_ref[...]).astype(BF16)
    kv = _dot(h, w_ref[...])
    for hd in range(H):
        kh = kv[:, hd * DH:(hd + 1) * DH]
        k_ref[:, hd * DH:(hd + 1) * DH] = (_rms(kh, DH) * gk_ref[...]).astype(BF16)
    v_ref[...] = kv[:, H * DH:].astype(BF16)


def _mem_kv(mem2d, norm_mem, w_xkv, g_k, *, bm=256):
    M, D = mem2d.shape
    HD = XATTN_HEADS * XATTN_HEAD_DIM
    bm = min(bm, M)
    const = lambda i: (0, 0)
    out = jax.ShapeDtypeStruct((M, HD), BF16)
    return pl.pallas_call(
        _mem_kv_kernel,
        grid=(M // bm,),
        in_specs=[pl.BlockSpec((bm, D), lambda i: (i, 0)), pl.BlockSpec((1, D), const),
                  pl.BlockSpec((D, 2 * HD), const), pl.BlockSpec((1, XATTN_HEAD_DIM), const)],
        out_specs=[pl.BlockSpec((bm, HD), lambda i: (i, 0)), pl.BlockSpec((bm, HD), lambda i: (i, 0))],
        out_shape=[out, out],
        compiler_params=_params("parallel"),
        name="xattn_mem_kv",
    )(mem2d, norm_mem.reshape(1, D), w_xkv.astype(BF16), g_k.reshape(1, -1))


def _xattn_kernel(x_ref, g_ref, wq_ref, gq_ref, k_ref, v_ref, wo_ref, o_ref, att_ref, *, scale):
    H, DH = XATTN_HEADS, XATTN_HEAD_DIM
    x = x_ref[...]
    h = (_rms(x, x.shape[-1]) * g_ref[...]).astype(BF16)
    q = _dot(h, wq_ref[...])
    for hd in range(H):
        sl = slice(hd * DH, (hd + 1) * DH)
        qh = (_rms(q[:, sl], DH) * gq_ref[...] * scale).astype(BF16)
        s = _dot_nt(qh, k_ref[:, sl])
        e = jnp.exp(s - jnp.max(s, axis=-1, keepdims=True))
        o = _dot(e.astype(BF16), v_ref[:, sl]) / jnp.sum(e, axis=-1, keepdims=True)
        att_ref[:, sl] = o.astype(BF16)
    o_ref[...] = x + _dot(att_ref[...], wo_ref[...])


def _xattn(x, norm_x, w_xq, g_q, k, v, w_xo, B, S, *, bm=512):
    T, D = x.shape
    HD = XATTN_HEADS * XATTN_HEAD_DIM
    M = k.shape[0] // B
    bm = min(bm, S)
    nt = S // bm
    const = lambda b_, i: (0, 0)
    return pl.pallas_call(
        functools.partial(_xattn_kernel, scale=float(XATTN_HEAD_DIM ** -0.5)),
        grid=(B, nt),
        in_specs=[
            pl.BlockSpec((bm, D), lambda b_, i: (b_ * nt + i, 0)),
            pl.BlockSpec((1, D), const),
            pl.BlockSpec((D, HD), const),
            pl.BlockSpec((1, XATTN_HEAD_DIM), const),
            pl.BlockSpec((M, HD), lambda b_, i: (b_, 0)),
            pl.BlockSpec((M, HD), lambda b_, i: (b_, 0)),
            pl.BlockSpec((HD, D), const),
        ],
        out_specs=pl.BlockSpec((bm, D), lambda b_, i: (b_ * nt + i, 0)),
        out_shape=jax.ShapeDtypeStruct((T, D), F32),
        scratch_shapes=[pltpu.VMEM((bm, HD), BF16)],
        compiler_params=_params("parallel", "arbitrary"),
        name="xattn",
    )(x, norm_x.reshape(1, D), w_xq.astype(BF16), g_q.reshape(1, -1), k, v, w_xo.astype(BF16))


def _layout_w_in(w_in, C, q_lora, kv_lora, D_ml, n_gate):
    o = np.cumsum([0, 2 * C, q_lora, kv_lora, ROPE_DIM, D_ml, D_ml, n_gate])
    conv, cq, ckv, kr, xm, z, gate = (w_in[:, o[t]:o[t + 1]] for t in range(7))
    segs = [gate, conv, xm, z, cq, ckv, _pad_lanes(kr), _pad_lanes(_swap_halves(kr))]
    w = jnp.concatenate(segs, axis=1)
    offs = np.cumsum([0] + [s.shape[1] for s in segs])
    pad = (-w.shape[1]) % 1024
    w = jnp.pad(w, ((0, 0), (0, pad)))
    cols = dict(gate=int(offs[0]), a=int(offs[1]), g=int(offs[1]) + C, xm=int(offs[2]), z=int(offs[3]),
                cq=int(offs[4]), ckv=int(offs[5]), kr=int(offs[6]), krs=int(offs[7]))
    return w.astype(BF16), cols


def kernel(x, mem, positions, norm_mix, w_in, b_gate, conv_dw, conv_dw_b, conv_ln_g, conv_ln_b, w_conv_out, mla_q_norm, mla_kv_norm, w_q_up, w_kv_up, mla_g_q, mla_g_k, w_mla_out, mlstm_conv_w, mlstm_conv_b, w_mq, w_mk, w_mv, w_if, b_if, mlstm_gn_g, mlstm_skip, w_mlstm_out, w_mix_out, norm_x, norm_mem, w_xq, w_xkv, xattn_g_q, xattn_g_k, w_xo, norm_ffn, w_ffn_in, w_ffn_out):
    B, S, D = x.shape
    T = B * S
    depth = w_in.shape[0]
    C = conv_dw.shape[2]
    D_ml = mlstm_conv_w.shape[2]
    L = min(MLSTM_CHUNK, S)
    xt = x.reshape(T, D)
    pos = positions.reshape(T, 1)
    mem2d = mem.reshape(-1, D)
    for l in range(depth):
        w_in_p, cols = _layout_w_in(w_in[l], C, w_q_up.shape[1], w_kv_up.shape[1], D_ml, b_gate.shape[1])
        proj = _matmul(xt, w_in_p, bm=512, bn=1024, gain=norm_mix[l], name="in_proj")
        c = _conv_branch(proj, cols["a"], cols["g"], C, B, S, conv_dw[l], conv_dw_b[l],
                         conv_ln_g[l], conv_ln_b[l])
        q, k, v = _mla_prep(proj, (cols["cq"], cols["ckv"], cols["kr"], cols["krs"]), pos,
                            mla_q_norm[l], mla_kv_norm[l], w_q_up[l], w_kv_up[l], mla_g_q[l], mla_g_k[l])
        a = _flash_attention(q, k, v, B, S)
        mq, mk, mv, xc, gc, gr = _mlstm_prep(proj, cols["xm"], B, S, mlstm_conv_w[l], mlstm_conv_b[l],
                                             w_mq[l], w_mk[l], w_mv[l], w_if[l], b_if[l], L=L)
        m = _mlstm_scan(mq, mk, mv, xc, proj, cols["z"], gc, gr, mlstm_gn_g[l], mlstm_skip[l], B, S, L=L)
        merged = _merge(c, a, m, proj, cols["gate"], b_gate[l], w_conv_out[l].astype(BF16),
                        w_mla_out[l].astype(BF16), w_mlstm_out[l].astype(BF16))
        xt = _matmul(merged, w_mix_out[l].astype(BF16), bm=512, bn=1024, residual=xt, out_dtype=F32,
                     name="mix_out")
        xk, xv = _mem_kv(mem2d, norm_mem[l], w_xkv[l], xattn_g_k[l])
        xt = _xattn(xt, norm_x[l], w_xq[l], xattn_g_q[l], xk, xv, w_xo[l], B, S)
        y = _matmul(xt, w_ffn_in[l].astype(BF16), bm=512, bn=512, gain=norm_ffn[l], swiglu=True,
                    name="ffn_in")
        xt = _matmul(y, w_ffn_out[l].astype(BF16), bm=512, bn=512, residual=xt, out_dtype=F32,
                     name="ffn_out")
    return xt.reshape(B, S, D)
```

```python
import functools

import jax
import jax.numpy as jnp
import numpy as np
from jax import lax
from jax.experimental import pallas as pl
from jax.experimental.pallas import tpu as pltpu

F32 = jnp.float32
BF16 = jnp.bfloat16

EPS = 1e-6
CONV_WIDTH = 31
MLA_HEADS = 8
NOPE_DIM = 128
ROPE_DIM = 64
V_DIM = 128
ROPE_BASE = 10000.0
MLSTM_HEADS = 4
MLSTM_CONV_WIDTH = 4
XATTN_HEADS = 4
XATTN_HEAD_DIM = 128
LANES = 128
SUBLANES = 8
NEG_BIG = -1e30
VMEM_LIMIT = 56 * 1024 * 1024

MLSTM_CHUNK = 256
CONV_HALO = 32
MLSTM_HALO = 16


def _params(*sem):
    return pltpu.CompilerParams(dimension_semantics=sem, vmem_limit_bytes=VMEM_LIMIT)


def _sigmoid(x):
    return 1.0 / (1.0 + jnp.exp(-x))


def _silu(x):
    return x * _sigmoid(x)


def _dot(a, b):
    return jnp.dot(a, b, preferred_element_type=F32)


def _dot_nt(a, b):
    return lax.dot_general(a, b, (((1,), (1,)), ((), ())), preferred_element_type=F32)


def _rms(x, n):
    ms = jnp.sum(x * x, axis=-1, keepdims=True) * (1.0 / n)
    return x * lax.rsqrt(ms + EPS)


def _mm_kernel(*refs, norm, swiglu, has_res):
    it = iter(refs)
    x_ref = next(it)
    g_ref = next(it) if norm else None
    w_ref = next(it)
    w2_ref = next(it) if swiglu else None
    r_ref = next(it) if has_res else None
    o_ref = next(it)
    h_ref = next(it) if norm else None
    if norm:
        @pl.when(pl.program_id(1) == 0)
        def _():
            xf = x_ref[...].astype(F32)
            h_ref[...] = (_rms(xf, xf.shape[-1]) * g_ref[...]).astype(BF16)
        h = h_ref[...]
    else:
        h = x_ref[...]
    acc = _dot(h, w_ref[...])
    if swiglu:
        acc = _silu(acc) * _dot(h, w2_ref[...])
    if has_res:
        acc = r_ref[...] + acc
    o_ref[...] = acc.astype(o_ref.dtype)


def _matmul(x, w, *, bm, bn, gain=None, swiglu=False, residual=None, out_dtype=BF16,
            x_cols=None, name="matmul"):
    M = x.shape[0]
    c0, K = (0, x.shape[1]) if x_cols is None else x_cols
    assert c0 % K == 0 and w.shape[0] == K
    N = w.shape[1] // 2 if swiglu else w.shape[1]
    bm, bn = min(bm, M), min(bn, N)
    assert M % bm == 0 and N % bn == 0
    xcb = c0 // K
    norm = gain is not None
    in_specs = [pl.BlockSpec((bm, K), lambda i, j: (i, xcb))]
    args = [x]
    if norm:
        in_specs.append(pl.BlockSpec((1, K), lambda i, j: (0, 0)))
        args.append(gain.reshape(1, K).astype(F32))
    in_specs.append(pl.BlockSpec((K, bn), lambda i, j: (0, j)))
    args.append(w)
    if swiglu:
        nb = N // bn
        in_specs.append(pl.BlockSpec((K, bn), lambda i, j: (0, j + nb)))
        args.append(w)
    if residual is not None:
        in_specs.append(pl.BlockSpec((bm, bn), lambda i, j: (i, j)))
        args.append(residual)
    scratch = [pltpu.VMEM((bm, K), BF16)] if norm else []
    return pl.pallas_call(
        functools.partial(_mm_kernel, norm=norm, swiglu=swiglu, has_res=residual is not None),
        grid=(M // bm, N // bn),
        in_specs=in_specs,
        out_specs=pl.BlockSpec((bm, bn), lambda i, j: (i, j)),
        out_shape=jax.ShapeDtypeStruct((M, N), out_dtype),
        scratch_shapes=scratch,
        compiler_params=_params("parallel", "arbitrary"),
        name=name,
    )(*args)


def _conv_kernel(a_ref, g_ref, w_ref, b_ref, lg_ref, lb_ref, o_ref, u_ref, sh_ref, *, bt, rt):
    i = pl.program_id(1)

    @pl.when(i == 0)
    def _():
        u_ref[0:CONV_HALO, :] = jnp.zeros((CONV_HALO, u_ref.shape[1]), F32)

    @pl.when(i > 0)
    def _():
        u_ref[0:CONV_HALO, :] = u_ref[bt:bt + CONV_HALO, :]

    a = a_ref[...].astype(F32)
    g = g_ref[...].astype(F32)
    u_ref[CONV_HALO:CONV_HALO + bt, :] = a * _sigmoid(g)
    n_sh = CONV_HALO + bt - SUBLANES
    for r in range(1, SUBLANES):
        sh_ref[r - 1, SUBLANES:, :] = u_ref[SUBLANES - r:SUBLANES - r + n_sh, :]
    bias = b_ref[...]
    for r0 in range(0, bt, rt):
        acc = jnp.broadcast_to(bias, (rt, bias.shape[1]))
        for j in range(CONV_WIDTH):
            shift = CONV_WIDTH - 1 - j
            r, start = shift % SUBLANES, r0 + CONV_HALO - (shift // SUBLANES) * SUBLANES
            win = u_ref[start:start + rt, :] if r == 0 else sh_ref[r - 1, start:start + rt, :]
            acc = acc + w_ref[j:j + 1, :] * win
        mu = jnp.mean(acc, axis=-1, keepdims=True)
        d = acc - mu
        var = jnp.mean(d * d, axis=-1, keepdims=True)
        y = d * lax.rsqrt(var + EPS) * lg_ref[...] + lb_ref[...]
        o_ref[r0:r0 + rt, :] = _silu(y).astype(o_ref.dtype)


def _conv_branch(proj, col_a, col_g, C, B, S, w, b, ln_g, ln_b, *, bt=256, rt=32):
    bt = min(bt, S)
    rt = min(rt, bt)
    nt = S // bt
    ca, cg = col_a // C, col_g // C
    return pl.pallas_call(
        functools.partial(_conv_kernel, bt=bt, rt=rt),
        grid=(B, nt),
        in_specs=[
            pl.BlockSpec((bt, C), lambda b_, i: (b_ * nt + i, ca)),
            pl.BlockSpec((bt, C), lambda b_, i: (b_ * nt + i, cg)),
            pl.BlockSpec((CONV_WIDTH, C), lambda b_, i: (0, 0)),
            pl.BlockSpec((1, C), lambda b_, i: (0, 0)),
            pl.BlockSpec((1, C), lambda b_, i: (0, 0)),
            pl.BlockSpec((1, C), lambda b_, i: (0, 0)),
        ],
        out_specs=pl.BlockSpec((bt, C), lambda b_, i: (b_ * nt + i, 0)),
        out_shape=jax.ShapeDtypeStruct((B * S, C), BF16),
        scratch_shapes=[pltpu.VMEM((CONV_HALO + bt, C), F32),
                        pltpu.VMEM((SUBLANES - 1, CONV_HALO + bt, C), F32)],
        compiler_params=_params("parallel", "arbitrary"),
        name="conformer_conv",
    )(proj, proj, w, b.reshape(1, C), ln_g.reshape(1, C), ln_b.reshape(1, C))


def _rope_kernel(pos_ref, freq_ref, sgn_ref, cs1_ref, cs2_ref):
    ang = pos_ref[...].astype(F32) * freq_ref[...]
    cs1_ref[...] = jnp.cos(ang)
    cs2_ref[...] = jnp.sin(ang) * sgn_ref[...]


def _rope_tables(pos, *, bm=1024):
    T = pos.shape[0]
    bm = min(bm, T)
    half = ROPE_DIM // 2
    inv_freq = ROPE_BASE ** (-jnp.arange(half, dtype=F32) / half)
    freq = _pad_lanes(jnp.concatenate([inv_freq, inv_freq]))[None, :]
    sgn = _pad_lanes(jnp.concatenate([-jnp.ones((half,), F32), jnp.ones((half,), F32)]))[None, :]
    const = lambda i: (0, 0)
    tab = jax.ShapeDtypeStruct((T, LANES), F32)
    return pl.pallas_call(
        _rope_kernel,
        grid=(T // bm,),
        in_specs=[pl.BlockSpec((bm, 1), lambda i: (i, 0)), pl.BlockSpec((1, LANES), const),
                  pl.BlockSpec((1, LANES), const)],
        out_specs=[pl.BlockSpec((bm, LANES), lambda i: (i, 0)), pl.BlockSpec((bm, LANES), lambda i: (i, 0))],
        out_shape=[tab, tab],
        compiler_params=_params("parallel"),
        name="rope_tables",
    )(pos, freq, sgn)


def _mla_prep_kernel(cq_ref, ckv_ref, kr_ref, krs_ref, cs1_ref, cs2_ref,
                     qn_ref, kvn_ref, wq_ref, wkv_ref, gq_ref, gk_ref,
                     q_ref, k_ref, v_ref, *, scale):
    H = MLA_HEADS
    cs1 = cs1_ref[...]
    cs2 = cs2_ref[...]

    cq = cq_ref[...].astype(F32)
    hq = (_rms(cq, cq.shape[-1]) * qn_ref[...]).astype(BF16)
    qall = _dot(hq, wq_ref[...])
    ckv = ckv_ref[...].astype(F32)
    hkv = (_rms(ckv, ckv.shape[-1]) * kvn_ref[...]).astype(BF16)
    kvall = _dot(hkv, wkv_ref[...])

    g_nope_q, g_rot_q, g_rots_q = gq_ref[0:1, :], gq_ref[1:2, :], gq_ref[2:3, :]
    g_nope_k, g_rot_k, g_rots_k = gk_ref[0:1, :], gk_ref[1:2, :], gk_ref[2:3, :]

    kr = kr_ref[...].astype(F32)
    krs = krs_ref[...].astype(F32)
    rs = lax.rsqrt(jnp.sum(kr * kr, axis=-1, keepdims=True) * (1.0 / ROPE_DIM) + EPS)
    k_rot = ((kr * rs * g_rot_k) * cs1 + (krs * rs * g_rots_k) * cs2).astype(BF16)

    for h in range(H):
        qn = qall[:, h * LANES:(h + 1) * LANES]
        q_ref[h, :, 0:LANES] = (_rms(qn, NOPE_DIM) * g_nope_q * scale).astype(BF16)
        qr = qall[:, (H + h) * LANES:(H + h + 1) * LANES]
        qs = qall[:, (2 * H + h) * LANES:(2 * H + h + 1) * LANES]
        rq = lax.rsqrt(jnp.sum(qr * qr, axis=-1, keepdims=True) * (1.0 / ROPE_DIM) + EPS)
        q_rot = (qr * rq * g_rot_q) * cs1 + (qs * rq * g_rots_q) * cs2
        q_ref[h, :, LANES:2 * LANES] = (q_rot * scale).astype(BF16)
        kn = kvall[:, h * LANES:(h + 1) * LANES]
        k_ref[h, :, 0:LANES] = (_rms(kn, NOPE_DIM) * g_nope_k).astype(BF16)
        k_ref[h, :, LANES:2 * LANES] = k_rot
        v_ref[h, :, 0:V_DIM] = kvall[:, (H + h) * LANES:(H + h + 1) * LANES].astype(BF16)
        v_ref[h, :, V_DIM:V_DIM + LANES] = jnp.ones((kvall.shape[0], LANES), BF16)


def _pad_lanes(v, n=LANES):
    return jnp.pad(v, [(0, 0)] * (v.ndim - 1) + [(0, n - v.shape[-1])])


def _swap_halves(v):
    half = v.shape[-1] // 2
    return jnp.concatenate([v[..., half:], v[..., :half]], axis=-1)


def _mla_prep(proj, cols, cs1, cs2, mla_q_norm, mla_kv_norm, w_q_up, w_kv_up, g_q, g_k, *, bm=512):
    T = proj.shape[0]
    bm = min(bm, T)
    H = MLA_HEADS
    q_lora, kv_lora = w_q_up.shape[0], w_kv_up.shape[0]
    c_cq, c_ckv, c_kr, c_krs = cols
    wq = w_q_up.reshape(q_lora, H, NOPE_DIM + ROPE_DIM)
    wq_rot = wq[:, :, NOPE_DIM:]
    wq_p = jnp.concatenate([
        wq[:, :, :NOPE_DIM].reshape(q_lora, H * NOPE_DIM),
        _pad_lanes(wq_rot).reshape(q_lora, H * LANES),
        _pad_lanes(_swap_halves(wq_rot)).reshape(q_lora, H * LANES)], axis=1).astype(BF16)
    wkv = w_kv_up.reshape(kv_lora, H, NOPE_DIM + V_DIM)
    wkv_p = jnp.concatenate([wkv[:, :, :NOPE_DIM].reshape(kv_lora, H * NOPE_DIM),
                             wkv[:, :, NOPE_DIM:].reshape(kv_lora, H * V_DIM)], axis=1).astype(BF16)

    def gains(g):
        return jnp.stack([g[:NOPE_DIM], _pad_lanes(g[NOPE_DIM:]), _pad_lanes(_swap_halves(g[NOPE_DIM:]))])

    scale = float((NOPE_DIM + ROPE_DIM) ** -0.5 * np.log2(np.e))
    const = lambda i: (0, 0)
    return pl.pallas_call(
        functools.partial(_mla_prep_kernel, scale=scale),
        grid=(T // bm,),
        in_specs=[
            pl.BlockSpec((bm, q_lora), lambda i: (i, c_cq // q_lora)),
            pl.BlockSpec((bm, kv_lora), lambda i: (i, c_ckv // kv_lora)),
            pl.BlockSpec((bm, LANES), lambda i: (i, c_kr // LANES)),
            pl.BlockSpec((bm, LANES), lambda i: (i, c_krs // LANES)),
            pl.BlockSpec((bm, LANES), lambda i: (i, 0)),
            pl.BlockSpec((bm, LANES), lambda i: (i, 0)),
            pl.BlockSpec((1, q_lora), const),
            pl.BlockSpec((1, kv_lora), const),
            pl.BlockSpec(wq_p.shape, const),
            pl.BlockSpec(wkv_p.shape, const),
            pl.BlockSpec((3, LANES), const),
            pl.BlockSpec((3, LANES), const),
        ],
        out_specs=[
            pl.BlockSpec((H, bm, 2 * LANES), lambda i: (0, i, 0)),
            pl.BlockSpec((H, bm, 2 * LANES), lambda i: (0, i, 0)),
            pl.BlockSpec((H, bm, V_DIM + LANES), lambda i: (0, i, 0)),
        ],
        out_shape=[
            jax.ShapeDtypeStruct((H, T, 2 * LANES), BF16),
            jax.ShapeDtypeStruct((H, T, 2 * LANES), BF16),
            jax.ShapeDtypeStruct((H, T, V_DIM + LANES), BF16),
        ],
        compiler_params=_params("parallel"),
        name="mla_prep",
    )(proj, proj, proj, proj, cs1, cs2, mla_q_norm.reshape(1, -1), mla_kv_norm.reshape(1, -1),
      wq_p, wkv_p, gains(g_q), gains(g_k))


def _flash_kernel(q_ref, k_ref, v_ref, o_ref, m_ref, acc_ref, *, bh, nh):
    i = pl.program_id(2)
    m_ref[...] = jnp.full(m_ref.shape, NEG_BIG, F32)
    acc_ref[...] = jnp.zeros(acc_ref.shape, F32)
    nc = bh // LANES

    def chain(a, j, masked):
        rows = slice(a * bh, (a + 1) * bh)
        start = pl.multiple_of(j * bh, bh)
        k = k_ref[0, pl.ds(start, bh), :]
        v = v_ref[0, pl.ds(start, bh), :]
        s = _dot_nt(q_ref[0, rows, :], k)
        if masked:
            row = lax.broadcasted_iota(jnp.int32, s.shape, 0)
            col = lax.broadcasted_iota(jnp.int32, s.shape, 1)
            s = jnp.where(col <= row, s, NEG_BIG)
        chunks = [s[:, c * LANES:(c + 1) * LANES] for c in range(nc)]
        mx = functools.reduce(jnp.maximum, chunks)
        m_prev = m_ref[rows, :]
        m_new = jnp.maximum(m_prev, jnp.max(mx, axis=-1, keepdims=True))
        alpha = jnp.exp2(m_prev - m_new)
        p = jnp.concatenate([jnp.exp2(c - m_new) for c in chunks], axis=1).astype(BF16)
        alpha2 = jnp.concatenate([alpha] * (acc_ref.shape[1] // LANES), axis=1)
        acc_ref[rows, :] = alpha2 * acc_ref[rows, :] + _dot(p, v)
        m_ref[rows, :] = m_new

    def body(j, carry):
        for a in range(nh):
            chain(a, j, False)
        return carry

    lax.fori_loop(0, i * nh, body, 0)
    for a in range(nh):
        for c in range(a + 1):
            chain(a, i * nh + c, c == a)
    o_ref[...] = (acc_ref[:, 0:V_DIM] / acc_ref[:, V_DIM:2 * V_DIM]).astype(o_ref.dtype)


def _flash_attention(q, k, v, B, S, *, bh=512, nh=2):
    H = q.shape[0]
    bh = min(bh, S // nh)
    bq = bh * nh
    nq = S // bq
    return pl.pallas_call(
        functools.partial(_flash_kernel, bh=bh, nh=nh),
        grid=(B, H, nq),
        in_specs=[
            pl.BlockSpec((1, bq, q.shape[2]), lambda b, h, i: (h, b * nq + i, 0)),
            pl.BlockSpec((1, S, k.shape[2]), lambda b, h, i: (h, b, 0)),
            pl.BlockSpec((1, S, v.shape[2]), lambda b, h, i: (h, b, 0)),
        ],
        out_specs=pl.BlockSpec((bq, V_DIM), lambda b, h, i: (b * nq + i, h)),
        out_shape=jax.ShapeDtypeStruct((B * S, H * V_DIM), BF16),
        scratch_shapes=[pltpu.VMEM((bq, LANES), F32), pltpu.VMEM((bq, v.shape[2]), F32)],
        compiler_params=_params("parallel", "parallel", "arbitrary"),
        name="mla_flash",
    )(q, k, v)


def _log_sigmoid(x):
    return -(jnp.maximum(-x, 0.0) + jnp.log(1.0 + jnp.exp(-jnp.abs(x))))


def _split_dot(a, b_exact):
    hi = a.astype(BF16)
    lo = (a - hi.astype(F32)).astype(BF16)
    return _dot(hi, b_exact) + _dot(lo, b_exact)


def _mlstm_prep_kernel(xm_ref, halo_ref, cw_ref, cb_ref, wq_ref, wk_ref, wv_ref, wif_ref, wift_ref,
                       bif_ref, bift_ref, q_ref, k_ref, v_ref, xc_ref, gc_ref, gr_ref,
                       buf_ref, qkv_ref, *, bm, L):
    i = pl.program_id(1)
    H = MLSTM_HEADS
    D = xm_ref.shape[1]
    DH = D // H
    halo = halo_ref[...].astype(F32)
    buf_ref[0:MLSTM_HALO, :] = jnp.where(i > 0, halo, 0.0)
    xm_bf = xm_ref[...]
    buf_ref[MLSTM_HALO:MLSTM_HALO + bm, :] = xm_bf.astype(F32)
    off = MLSTM_HALO - (MLSTM_CONV_WIDTH - 1)
    acc = jnp.broadcast_to(cb_ref[...], (bm, D))
    for j in range(MLSTM_CONV_WIDTH):
        acc = acc + cw_ref[j:j + 1, :] * buf_ref[off + j:off + j + bm, :]
    xc = _silu(acc)
    xc_bf = xc.astype(BF16)
    xc_ref[...] = xc_bf
    for h in range(H):
        sl = slice(h * DH, (h + 1) * DH)
        qh = _dot(xc_bf[:, sl], wq_ref[h]).astype(BF16)
        k_raw = _dot(xc_bf[:, sl], wk_ref[h])
        vh = _dot(xm_bf[:, sl], wv_ref[h]).astype(BF16)
        q_ref[:, sl] = qh
        k_ref[:, sl] = (k_raw * (DH ** -0.5)).astype(BF16)
        v_ref[:, sl] = vh
        qkv_ref[:, h * DH:(h + 1) * DH] = qh
        qkv_ref[:, D + h * DH:D + (h + 1) * DH] = k_raw.astype(BF16)
        qkv_ref[:, 2 * D + h * DH:2 * D + (h + 1) * DH] = vh
    qkv = qkv_ref[...]
    pre_c = _dot(qkv, wif_ref[...]) + bif_ref[...]
    lane = lax.broadcasted_iota(jnp.int32, pre_c.shape, 1)
    is_f_c = (lane >= H) & (lane < 2 * H)
    lf_c = jnp.where(is_f_c, _log_sigmoid(pre_c), 0.0)
    r = lax.broadcasted_iota(jnp.int32, (bm, bm), 0)
    c = lax.broadcasted_iota(jnp.int32, (bm, bm), 1)
    same = (r // L) == (c // L)
    tri = jnp.where(same & (c <= r), 1.0, 0.0).astype(BF16)
    cum_c = _split_dot_left(tri, lf_c)
    gc_ref[...] = jnp.where(is_f_c, cum_c, pre_c)
    pre_r = _dot_nt(wift_ref[...], qkv) + bift_ref[...]
    row = lax.broadcasted_iota(jnp.int32, pre_r.shape, 0)
    is_f_r = row >= H
    lf_r = jnp.where(is_f_r, _log_sigmoid(pre_r), 0.0)
    trit = jnp.where(same & (r <= c), 1.0, 0.0).astype(BF16)
    cum_r = _split_dot(lf_r, trit)
    gr_ref[...] = jnp.where(is_f_r, cum_r, pre_r)


def _split_dot_left(a_exact, b):
    hi = b.astype(BF16)
    lo = (b - hi.astype(F32)).astype(BF16)
    return _dot(a_exact, hi) + _dot(a_exact, lo)


def _mlstm_prep(proj, col_xm, B, S, conv_w, conv_b, w_mq, w_mk, w_mv, w_if, b_if, *, bm=512, L):
    T = B * S
    H = MLSTM_HEADS
    D = conv_w.shape[1]
    bm = min(bm, S)
    assert bm % L == 0
    nt = S // bm
    cxm = col_xm // D
    hb = bm // MLSTM_HALO
    wif = _pad_lanes(w_if).astype(BF16)
    wift = w_if.T.astype(BF16)
    bif = _pad_lanes(b_if)[None, :]
    bift = b_if[:, None]
    const2 = lambda b_, i: (0, 0)
    const3 = lambda b_, i: (0, 0, 0)
    row_blk = lambda b_, i: (b_ * nt + i, 0)
    act = jax.ShapeDtypeStruct((T, D), BF16)
    return pl.pallas_call(
        functools.partial(_mlstm_prep_kernel, bm=bm, L=L),
        grid=(B, nt),
        in_specs=[
            pl.BlockSpec((bm, D), lambda b_, i: (b_ * nt + i, cxm)),
            pl.BlockSpec((MLSTM_HALO, D), lambda b_, i: (jnp.maximum((b_ * nt + i) * hb - 1, 0), cxm)),
            pl.BlockSpec((MLSTM_CONV_WIDTH, D), const2),
            pl.BlockSpec((1, D), const2),
            pl.BlockSpec(w_mq.shape, const3),
            pl.BlockSpec(w_mk.shape, const3),
            pl.BlockSpec(w_mv.shape, const3),
            pl.BlockSpec(wif.shape, const2),
            pl.BlockSpec(wift.shape, const2),
            pl.BlockSpec((1, LANES), const2),
            pl.BlockSpec((2 * H, 1), const2),
        ],
        out_specs=[
            pl.BlockSpec((bm, D), row_blk), pl.BlockSpec((bm, D), row_blk),
            pl.BlockSpec((bm, D), row_blk), pl.BlockSpec((bm, D), row_blk),
            pl.BlockSpec((bm, LANES), row_blk),
            pl.BlockSpec((2 * H, bm), lambda b_, i: (0, b_ * nt + i)),
        ],
        out_shape=[act, act, act, act,
                   jax.ShapeDtypeStruct((T, LANES), F32),
                   jax.ShapeDtypeStruct((2 * H, T), F32)],
        scratch_shapes=[pltpu.VMEM((MLSTM_HALO + bm, D), F32), pltpu.VMEM((bm, 3 * D), BF16)],
        compiler_params=_params("parallel", "arbitrary"),
        name="mlstm_prep",
    )(proj, proj, conv_w, conv_b.reshape(1, D), w_mq.astype(BF16), w_mk.astype(BF16),
      w_mv.astype(BF16), wif, wift, bif, bift)


def _mlstm_scan_kernel(q_ref, k_ref, v_ref, xc_ref, z_ref, gc_ref, gr_ref, gn_ref, skip_ref,
                       o_ref, c_ref, n_ref, m_ref, *, L):
    H = MLSTM_HEADS
    DH = q_ref.shape[1] // H

    @pl.when(pl.program_id(1) == 0)
    def _():
        c_ref[...] = jnp.zeros(c_ref.shape, F32)
        n_ref[...] = jnp.zeros(n_ref.shape, F32)
        m_ref[...] = jnp.zeros(m_ref.shape, F32)

    row = lax.broadcasted_iota(jnp.int32, (L, L), 0)
    col = lax.broadcasted_iota(jnp.int32, (L, L), 1)
    causal = col <= row
    for h in range(H):
        sl = slice(h * DH, (h + 1) * DH)
        q = q_ref[:, sl]
        k = k_ref[:, sl]
        v = v_ref[:, sl]
        li_c = gc_ref[:, h:h + 1]
        b_c = gc_ref[:, H + h:H + h + 1]
        li_r = gr_ref[h:h + 1, :]
        b_r = gr_ref[H + h:H + h + 1, :]
        m_prev = m_ref[h, 0:1, 0:1]
        C = c_ref[h]
        n = n_ref[h]

        dlog = jnp.where(causal, b_c - b_r + li_r, NEG_BIG)
        inter = b_c + m_prev
        m_comb = jnp.maximum(inter, jnp.max(dlog, axis=-1, keepdims=True))
        w_intra = jnp.exp(dlog - m_comb)
        w_inter = jnp.exp(inter - m_comb)
        s = _dot_nt(q, k) * w_intra
        num = w_inter * _dot(q, C.astype(BF16)) + _dot(s.astype(BF16), v)
        qf = q.astype(F32)
        den = w_inter * jnp.sum(qf * n, axis=-1, keepdims=True) + jnp.sum(s, axis=-1, keepdims=True)
        hh = num / jnp.maximum(jnp.abs(den), jnp.exp(-m_comb))

        b_last = b_c[L - 1:L, :]
        g_c = b_last - b_c + li_c
        m_new = jnp.maximum(b_last + m_prev, jnp.max(g_c, axis=0, keepdims=True))
        wg_c = jnp.exp(g_c - m_new)
        decay = jnp.exp(b_last + m_prev - m_new)
        kw = k.astype(F32) * wg_c
        c_ref[h] = decay * C + _dot(kw.T.astype(BF16), v)
        n_ref[h] = decay * n + jnp.sum(kw, axis=0, keepdims=True)
        m_ref[h] = jnp.broadcast_to(m_new, m_ref.shape[1:])

        mu = jnp.mean(hh, axis=-1, keepdims=True)
        d = hh - mu
        var = jnp.mean(d * d, axis=-1, keepdims=True)
        hn = d * lax.rsqrt(var + EPS) * gn_ref[:, sl]
        hn = hn + skip_ref[:, sl] * xc_ref[:, sl].astype(F32)
        o_ref[:, sl] = (hn * _silu(z_ref[:, sl].astype(F32))).astype(o_ref.dtype)


def _mlstm_scan(q, k, v, xc, proj, col_z, gc, gr, gn_g, skip, B, S, *, L):
    T, D = q.shape
    H = MLSTM_HEADS
    DH = D // H
    nc = S // L
    cz = col_z // D
    row_blk = lambda b_, i: (b_ * nc + i, 0)
    const2 = lambda b_, i: (0, 0)
    return pl.pallas_call(
        functools.partial(_mlstm_scan_kernel, L=L),
        grid=(B, nc),
        in_specs=[
            pl.BlockSpec((L, D), row_blk), pl.BlockSpec((L, D), row_blk),
            pl.BlockSpec((L, D), row_blk), pl.BlockSpec((L, D), row_blk),
            pl.BlockSpec((L, D), lambda b_, i: (b_ * nc + i, cz)),
            pl.BlockSpec((L, LANES), row_blk),
            pl.BlockSpec((2 * H, L), lambda b_, i: (0, b_ * nc + i)),
            pl.BlockSpec((1, D), const2),
            pl.BlockSpec((1, D), const2),
        ],
        out_specs=pl.BlockSpec((L, D), row_blk),
        out_shape=jax.ShapeDtypeStruct((T, D), BF16),
        scratch_shapes=[pltpu.VMEM((H, DH, DH), F32), pltpu.VMEM((H, 1, DH), F32),
                        pltpu.VMEM((H, 8, LANES), F32)],
        compiler_params=_params("parallel", "arbitrary"),
        name="mlstm_scan",
    )(q, k, v, xc, proj, gc, gr, gn_g.reshape(1, D), skip.reshape(1, D))


def _merge_mix_kernel(c_ref, a_ref, m_ref, g_ref, bg_ref, wc_ref, wa_ref, wm_ref, wmix_ref, x_ref, o_ref):
    N = x_ref.shape[1]

    def gate(t):
        return _sigmoid(g_ref[:, t * N:(t + 1) * N].astype(F32) + bg_ref[:, t * N:(t + 1) * N])

    y = gate(0) * _dot(c_ref[...], wc_ref[...])
    y = y + gate(1) * _dot(a_ref[...], wa_ref[...])
    y = y + gate(2) * _dot(m_ref[...], wm_ref[...])
    o_ref[...] = x_ref[...] + _dot(y.astype(BF16), wmix_ref[...])


def _merge_mix(c, a, m, proj, col_gate, b_gate, wc, wa, wm, wmix, x, *, bm=256):
    T, K = c.shape
    N = wc.shape[1]
    bm = min(bm, T)
    assert col_gate % (3 * N) == 0
    gcb = col_gate // (3 * N)
    row = lambda i: (i, 0)
    const = lambda i: (0, 0)
    xs = pl.BlockSpec((bm, K), row)
    ws = pl.BlockSpec((K, N), const, pipeline_mode=pl.Buffered(1))
    return pl.pallas_call(
        _merge_mix_kernel,
        grid=(T // bm,),
        in_specs=[xs, xs, xs,
                  pl.BlockSpec((bm, 3 * N), lambda i: (i, gcb)),
                  pl.BlockSpec((1, 3 * N), const),
                  ws, ws, ws,
                  pl.BlockSpec((N, N), const, pipeline_mode=pl.Buffered(1)),
                  pl.BlockSpec((bm, N), row)],
        out_specs=pl.BlockSpec((bm, N), row),
        out_shape=jax.ShapeDtypeStruct((T, N), F32),
        compiler_params=_params("parallel"),
        name="merge_mix",
    )(c, a, m, proj, b_gate.reshape(1, 3 * N), wc, wa, wm, wmix, x)


def _ffn_kernel(x_ref, g_ref, wg_ref, wu_ref, wo_ref, o_ref, h_ref):
    @pl.when(pl.program_id(1) == 0)
    def _():
        xf = x_ref[...]
        h_ref[...] = (_rms(xf, xf.shape[-1]) * g_ref[...]).astype(BF16)
        o_ref[...] = xf

    h = h_ref[...]
    y = (_silu(_dot(h, wg_ref[...])) * _dot(h, wu_ref[...])).astype(BF16)
    o_ref[...] += _dot(y, wo_ref[...])


def _ffn(x, gain, w_in, w_out, *, bm=1024, bh=512):
    T, D = x.shape
    F = w_out.shape[0]
    bm, bh = min(bm, T), min(bh, F)
    nh = F // bh
    return pl.pallas_call(
        _ffn_kernel,
        grid=(T // bm, nh),
        in_specs=[pl.BlockSpec((bm, D), lambda i, j: (i, 0)),
                  pl.BlockSpec((1, D), lambda i, j: (0, 0)),
                  pl.BlockSpec((D, bh), lambda i, j: (0, j)),
                  pl.BlockSpec((D, bh), lambda i, j: (0, j + nh)),
                  pl.BlockSpec((bh, D), lambda i, j: (j, 0))],
        out_specs=pl.BlockSpec((bm, D), lambda i, j: (i, 0)),
        out_shape=jax.ShapeDtypeStruct((T, D), F32),
        scratch_shapes=[pltpu.VMEM((bm, D), BF16)],
        compiler_params=_params("parallel", "arbitrary"),
        name="ffn",
    )(x, gain.reshape(1, D).astype(F32), w_in, w_in, w_out)


def _mem_kv_kernel(mem_ref, g_ref, w_ref, gk_ref, k_ref, v_ref):
    H, DH = XATTN_HEADS, XATTN_HEAD_DIM
    x = mem_ref[...]
    h = (_rms(x, x.shape[-1]) * g_ref[...]).astype(BF16)
    kv = _dot(h, w_ref[...])
    for hd in range(H):
        kh = kv[:, hd * DH:(hd + 1) * DH]
        k_ref[:, hd * DH:(hd + 1) * DH] = (_rms(kh, DH) * gk_ref[...]).astype(BF16)
    v_ref[...] = kv[:, H * DH:].astype(BF16)


def _mem_kv(mem2d, norm_mem, w_xkv, g_k, *, bm=256):
    M, D = mem2d.shape
    HD = XATTN_HEADS * XATTN_HEAD_DIM
    bm = min(bm, M)
    const = lambda i: (0, 0)
    out = jax.ShapeDtypeStruct((M, HD), BF16)
    return pl.pallas_call(
        _mem_kv_kernel,
        grid=(M // bm,),
        in_specs=[pl.BlockSpec((bm, D), lambda i: (i, 0)), pl.BlockSpec((1, D), const),
                  pl.BlockSpec((D, 2 * HD), const), pl.BlockSpec((1, XATTN_HEAD_DIM), const)],
        out_specs=[pl.BlockSpec((bm, HD), lambda i: (i, 0)), pl.BlockSpec((bm, HD), lambda i: (i, 0))],
        out_shape=[out, out],
        compiler_params=_params("parallel"),
        name="xattn_mem_kv",
    )(mem2d, norm_mem.reshape(1, D), w_xkv.astype(BF16), g_k.reshape(1, -1))


def _xattn_kernel(x_ref, g_ref, wq_ref, gq_ref, k_ref, v_ref, wo_ref, o_ref, att_ref, *, scale):
    H, DH = XATTN_HEADS, XATTN_HEAD_DIM
    x = x_ref[...]
    h = (_rms(x, x.shape[-1]) * g_ref[...]).astype(BF16)
    q = _dot(h, wq_ref[...])
    for hd in range(H):
        sl = slice(hd * DH, (hd + 1) * DH)
        qh = (_rms(q[:, sl], DH) * gq_ref[...] * scale).astype(BF16)
        s = _dot_nt(qh, k_ref[:, sl])
        e = jnp.exp(s - jnp.max(s, axis=-1, keepdims=True))
        o = _dot(e.astype(BF16), v_ref[:, sl]) / jnp.sum(e, axis=-1, keepdims=True)
        att_ref[:, sl] = o.astype(BF16)
    o_ref[...] = x + _dot(att_ref[...], wo_ref[...])


def _xattn(x, norm_x, w_xq, g_q, k, v, w_xo, B, S, *, bm=512):
    T, D = x.shape
    HD = XATTN_HEADS * XATTN_HEAD_DIM
    M = k.shape[0] // B
    bm = min(bm, S)
    nt = S // bm
    const = lambda b_, i: (0, 0)
    return pl.pallas_call(
        functools.partial(_xattn_kernel, scale=float(XATTN_HEAD_DIM ** -0.5)),
        grid=(B, nt),
        in_specs=[
            pl.BlockSpec((bm, D), lambda b_, i: (b_ * nt + i, 0)),
            pl.BlockSpec((1, D), const),
            pl.BlockSpec((D, HD), const),
            pl.BlockSpec((1, XATTN_HEAD_DIM), const),
            pl.BlockSpec((M, HD), lambda b_, i: (b_, 0)),
            pl.BlockSpec((M, HD), lambda b_, i: (b_, 0)),
            pl.BlockSpec((HD, D), const),
        ],
        out_specs=pl.BlockSpec((bm, D), lambda b_, i: (b_ * nt + i, 0)),
        out_shape=jax.ShapeDtypeStruct((T, D), F32),
        scratch_shapes=[pltpu.VMEM((bm, HD), BF16)],
        compiler_params=_params("parallel", "arbitrary"),
        name="xattn",
    )(x, norm_x.reshape(1, D), w_xq.astype(BF16), g_q.reshape(1, -1), k, v, w_xo.astype(BF16))


def _layout_w_in(w_in, C, q_lora, kv_lora, D_ml, n_gate):
    w = w_in.astype(BF16)
    o = np.cumsum([0, 2 * C, q_lora, kv_lora, ROPE_DIM, D_ml, D_ml, n_gate])
    conv, cq, ckv, kr, xm, z, gate = (w[..., o[t]:o[t + 1]] for t in range(7))
    half = ROPE_DIM // 2
    zpad = jnp.zeros(w.shape[:-1] + (LANES - ROPE_DIM,), BF16)
    segs = [gate, conv, xm, z, cq, ckv, kr, zpad, kr[..., half:], kr[..., :half], zpad]
    widths = [s.shape[-1] for s in segs]
    tail = (-sum(widths)) % 1024
    if tail:
        segs.append(jnp.zeros(w.shape[:-1] + (tail,), BF16))
    offs = np.cumsum([0] + widths)
    cols = dict(gate=int(offs[0]), a=int(offs[1]), g=int(offs[1]) + C, xm=int(offs[2]), z=int(offs[3]),
                cq=int(offs[4]), ckv=int(offs[5]), kr=int(offs[6]), krs=int(offs[8]))
    return jnp.concatenate(segs, axis=-1), cols


def kernel(x, mem, positions, norm_mix, w_in, b_gate, conv_dw, conv_dw_b, conv_ln_g, conv_ln_b, w_conv_out, mla_q_norm, mla_kv_norm, w_q_up, w_kv_up, mla_g_q, mla_g_k, w_mla_out, mlstm_conv_w, mlstm_conv_b, w_mq, w_mk, w_mv, w_if, b_if, mlstm_gn_g, mlstm_skip, w_mlstm_out, w_mix_out, norm_x, norm_mem, w_xq, w_xkv, xattn_g_q, xattn_g_k, w_xo, norm_ffn, w_ffn_in, w_ffn_out):
    B, S, D = x.shape
    T = B * S
    depth = w_in.shape[0]
    C = conv_dw.shape[2]
    D_ml = mlstm_conv_w.shape[2]
    L = min(MLSTM_CHUNK, S)
    xt = x.reshape(T, D)
    mem2d = mem.reshape(-1, D)
    cs1, cs2 = _rope_tables(positions.reshape(T, 1))
    w_in_p, cols = _layout_w_in(w_in, C, w_q_up.shape[1], w_kv_up.shape[1], D_ml, b_gate.shape[1])
    w_conv_out, w_mla_out, w_mlstm_out, w_mix_out, w_ffn_in, w_ffn_out = (
        w.astype(BF16) for w in (w_conv_out, w_mla_out, w_mlstm_out, w_mix_out, w_ffn_in, w_ffn_out))
    for l in range(depth):
        proj = _matmul(xt, w_in_p[l], bm=1024, bn=1024, gain=norm_mix[l], name="in_proj")
        c = _conv_branch(proj, cols["a"], cols["g"], C, B, S, conv_dw[l], conv_dw_b[l],
                         conv_ln_g[l], conv_ln_b[l])
        q, k, v = _mla_prep(proj, (cols["cq"], cols["ckv"], cols["kr"], cols["krs"]), cs1, cs2,
                            mla_q_norm[l], mla_kv_norm[l], w_q_up[l], w_kv_up[l], mla_g_q[l], mla_g_k[l])
        a = _flash_attention(q, k, v, B, S, nh=2 if l == 0 else 4)
        mq, mk, mv, xc, gc, gr = _mlstm_prep(proj, cols["xm"], B, S, mlstm_conv_w[l], mlstm_conv_b[l],
                                             w_mq[l], w_mk[l], w_mv[l], w_if[l], b_if[l], L=L)
        m = _mlstm_scan(mq, mk, mv, xc, proj, cols["z"], gc, gr, mlstm_gn_g[l], mlstm_skip[l], B, S, L=L)
        xt = _merge_mix(c, a, m, proj, cols["gate"], b_gate[l], w_conv_out[l], w_mla_out[l],
                        w_mlstm_out[l], w_mix_out[l], xt)
        xk, xv = _mem_kv(mem2d, norm_mem[l], w_xkv[l], xattn_g_k[l])
        xt = _xattn(xt, norm_x[l], w_xq[l], xattn_g_q[l], xk, xv, w_xo[l], B, S)
        if l == 0:
            xt = _ffn(xt, norm_ffn[l], w_ffn_in[l], w_ffn_out[l])
        else:
            y = _matmul(xt, w_ffn_in[l], bm=1024, bn=512, gain=norm_ffn[l], swiglu=True, name="ffn_in")
            xt = _matmul(y, w_ffn_out[l], bm=1024, bn=512, residual=xt, out_dtype=F32, name="ffn_out")
    return xt.reshape(B, S, D)
```

```python
import functools

import jax
import jax.numpy as jnp
import numpy as np
from jax import lax
from jax.experimental import pallas as pl
from jax.experimental.pallas import tpu as pltpu

F32 = jnp.float32
BF16 = jnp.bfloat16

EPS = 1e-6
CONV_WIDTH = 31
MLA_HEADS = 8
NOPE_DIM = 128
ROPE_DIM = 64
V_DIM = 128
ROPE_BASE = 10000.0
MLSTM_HEADS = 4
MLSTM_CONV_WIDTH = 4
XATTN_HEADS = 4
XATTN_HEAD_DIM = 128
LANES = 128
SUBLANES = 8
NEG_BIG = -1e30
VMEM_LIMIT = 56 * 1024 * 1024

MLSTM_CHUNK = 256
CONV_HALO = 32
MLSTM_HALO = 16


def _params(*sem):
    return pltpu.CompilerParams(dimension_semantics=sem, vmem_limit_bytes=VMEM_LIMIT)


def _sigmoid(x):
    return 0.5 * jnp.tanh(0.5 * x) + 0.5


def _silu(x):
    return x * _sigmoid(x)


def _dot(a, b):
    return jnp.dot(a, b, preferred_element_type=F32)


def _dot_nt(a, b):
    return lax.dot_general(a, b, (((1,), (1,)), ((), ())), preferred_element_type=F32)


def _rms(x, n):
    ms = jnp.sum(x * x, axis=-1, keepdims=True) * (1.0 / n)
    return x * lax.rsqrt(ms + EPS)


def _mm_kernel(x_ref, w_ref, o_ref):
    o_ref[...] = _dot(x_ref[...], w_ref[...]).astype(o_ref.dtype)


def _matmul(x, w, layer, *, bm, bn, name):
    M, K = x.shape
    N = w.shape[2]
    bm, bn = min(bm, M), min(bn, N)
    assert M % bm == 0 and N % bn == 0 and w.shape[1] == K
    return pl.pallas_call(
        _mm_kernel,
        grid=(M // bm, N // bn),
        in_specs=[pl.BlockSpec((bm, K), lambda i, j: (i, 0)),
                  pl.BlockSpec((None, K, bn), lambda i, j: (layer, 0, j))],
        out_specs=pl.BlockSpec((bm, bn), lambda i, j: (i, j)),
        out_shape=jax.ShapeDtypeStruct((M, N), BF16),
        compiler_params=_params("parallel", "arbitrary"),
        name=name,
    )(x, w)


def _conv_kernel(a_ref, g_ref, w_ref, b_ref, lg_ref, lb_ref, o_ref, u_ref, sh_ref, *, bt, rt):
    i = pl.program_id(1)

    @pl.when(i == 0)
    def _():
        u_ref[0:CONV_HALO, :] = jnp.zeros((CONV_HALO, u_ref.shape[1]), F32)

    @pl.when(i > 0)
    def _():
        u_ref[0:CONV_HALO, :] = u_ref[bt:bt + CONV_HALO, :]

    a = a_ref[...].astype(F32)
    g = g_ref[...].astype(F32)
    u_ref[CONV_HALO:CONV_HALO + bt, :] = a * _sigmoid(g)
    n_sh = CONV_HALO + bt - SUBLANES
    for r in range(1, SUBLANES):
        sh_ref[r - 1, SUBLANES:, :] = u_ref[SUBLANES - r:SUBLANES - r + n_sh, :]
    bias = b_ref[...]
    for r0 in range(0, bt, rt):
        acc = jnp.broadcast_to(bias, (rt, bias.shape[1]))
        for j in range(CONV_WIDTH):
            shift = CONV_WIDTH - 1 - j
            r, start = shift % SUBLANES, r0 + CONV_HALO - (shift // SUBLANES) * SUBLANES
            win = u_ref[start:start + rt, :] if r == 0 else sh_ref[r - 1, start:start + rt, :]
            acc = acc + w_ref[j:j + 1, :] * win
        mu = jnp.mean(acc, axis=-1, keepdims=True)
        d = acc - mu
        var = jnp.mean(d * d, axis=-1, keepdims=True)
        y = d * lax.rsqrt(var + EPS) * lg_ref[...] + lb_ref[...]
        o_ref[r0:r0 + rt, :] = _silu(y).astype(o_ref.dtype)


def _conv_branch(proj, col_a, col_g, C, B, S, w, b, ln_g, ln_b, *, bt=256, rt=32):
    bt = min(bt, S)
    rt = min(rt, bt)
    nt = S // bt
    ca, cg = col_a // C, col_g // C
    return pl.pallas_call(
        functools.partial(_conv_kernel, bt=bt, rt=rt),
        grid=(B, nt),
        in_specs=[
            pl.BlockSpec((bt, C), lambda b_, i: (b_ * nt + i, ca)),
            pl.BlockSpec((bt, C), lambda b_, i: (b_ * nt + i, cg)),
            pl.BlockSpec((CONV_WIDTH, C), lambda b_, i: (0, 0)),
            pl.BlockSpec((1, C), lambda b_, i: (0, 0)),
            pl.BlockSpec((1, C), lambda b_, i: (0, 0)),
            pl.BlockSpec((1, C), lambda b_, i: (0, 0)),
        ],
        out_specs=pl.BlockSpec((bt, C), lambda b_, i: (b_ * nt + i, 0)),
        out_shape=jax.ShapeDtypeStruct((B * S, C), BF16),
        scratch_shapes=[pltpu.VMEM((CONV_HALO + bt, C), F32),
                        pltpu.VMEM((SUBLANES - 1, CONV_HALO + bt, C), F32)],
        compiler_params=_params("parallel", "arbitrary"),
        name="conformer_conv",
    )(proj, proj, w, b.reshape(1, C), ln_g.reshape(1, C), ln_b.reshape(1, C))


def _rope_kernel(pos_ref, freq_ref, sgn_ref, cs1_ref, cs2_ref):
    ang = pos_ref[...].astype(F32) * freq_ref[...]
    cs1_ref[...] = jnp.cos(ang)
    cs2_ref[...] = jnp.sin(ang) * sgn_ref[...]


def _rope_tables(pos, *, bm=1024):
    T = pos.shape[0]
    bm = min(bm, T)
    half = ROPE_DIM // 2
    inv_freq = ROPE_BASE ** (-jnp.arange(half, dtype=F32) / half)
    freq = _pad_lanes(jnp.concatenate([inv_freq, inv_freq]))[None, :]
    sgn = _pad_lanes(jnp.concatenate([-jnp.ones((half,), F32), jnp.ones((half,), F32)]))[None, :]
    const = lambda i: (0, 0)
    tab = jax.ShapeDtypeStruct((T, LANES), F32)
    return pl.pallas_call(
        _rope_kernel,
        grid=(T // bm,),
        in_specs=[pl.BlockSpec((bm, 1), lambda i: (i, 0)), pl.BlockSpec((1, LANES), const),
                  pl.BlockSpec((1, LANES), const)],
        out_specs=[pl.BlockSpec((bm, LANES), lambda i: (i, 0)), pl.BlockSpec((bm, LANES), lambda i: (i, 0))],
        out_shape=[tab, tab],
        compiler_params=_params("parallel"),
        name="rope_tables",
    )(pos, freq, sgn)


def _mla_prep_kernel(cq_ref, ckv_ref, kr_ref, krs_ref, cs1_ref, cs2_ref,
                     qn_ref, kvn_ref, wq_ref, wkv_ref, gq_ref, gk_ref,
                     q_ref, k_ref, v_ref, *, scale):
    H = MLA_HEADS
    cs1 = cs1_ref[...]
    cs2 = cs2_ref[...]

    cq = cq_ref[...].astype(F32)
    hq = (_rms(cq, cq.shape[-1]) * qn_ref[...]).astype(BF16)
    qall = _dot(hq, wq_ref[...])
    ckv = ckv_ref[...].astype(F32)
    hkv = (_rms(ckv, ckv.shape[-1]) * kvn_ref[...]).astype(BF16)
    kvall = _dot(hkv, wkv_ref[...])

    g_nope_q, g_rot_q, g_rots_q = gq_ref[0:1, :], gq_ref[1:2, :], gq_ref[2:3, :]
    g_nope_k, g_rot_k, g_rots_k = gk_ref[0:1, :], gk_ref[1:2, :], gk_ref[2:3, :]

    kr = kr_ref[...].astype(F32)
    krs = krs_ref[...].astype(F32)
    rs = lax.rsqrt(jnp.sum(kr * kr, axis=-1, keepdims=True) * (1.0 / ROPE_DIM) + EPS)
    k_rot = ((kr * rs * g_rot_k) * cs1 + (krs * rs * g_rots_k) * cs2).astype(BF16)

    for h in range(H):
        qn = qall[:, h * LANES:(h + 1) * LANES]
        q_ref[h, :, 0:LANES] = (_rms(qn, NOPE_DIM) * g_nope_q * scale).astype(BF16)
        qr = qall[:, (H + h) * LANES:(H + h + 1) * LANES]
        qs = qall[:, (2 * H + h) * LANES:(2 * H + h + 1) * LANES]
        rq = lax.rsqrt(jnp.sum(qr * qr, axis=-1, keepdims=True) * (1.0 / ROPE_DIM) + EPS)
        q_rot = (qr * rq * g_rot_q) * cs1 + (qs * rq * g_rots_q) * cs2
        q_ref[h, :, LANES:2 * LANES] = (q_rot * scale).astype(BF16)
        kn = kvall[:, h * LANES:(h + 1) * LANES]
        k_ref[h, :, 0:LANES] = (_rms(kn, NOPE_DIM) * g_nope_k).astype(BF16)
        k_ref[h, :, LANES:2 * LANES] = k_rot
        v_ref[h, :, 0:V_DIM] = kvall[:, (H + h) * LANES:(H + h + 1) * LANES].astype(BF16)
        v_ref[h, :, V_DIM:V_DIM + LANES] = jnp.ones((kvall.shape[0], LANES), BF16)


def _pad_lanes(v, n=LANES):
    return jnp.pad(v, [(0, 0)] * (v.ndim - 1) + [(0, n - v.shape[-1])])


def _swap_halves(v):
    half = v.shape[-1] // 2
    return jnp.concatenate([v[..., half:], v[..., :half]], axis=-1)


def _mla_prep(proj, cols, cs1, cs2, mla_q_norm, mla_kv_norm, w_q_up, w_kv_up, g_q, g_k, *, bm=512):
    T = proj.shape[0]
    bm = min(bm, T)
    H = MLA_HEADS
    q_lora, kv_lora = w_q_up.shape[0], w_kv_up.shape[0]
    c_cq, c_ckv, c_kr, c_krs = cols
    wq = w_q_up.reshape(q_lora, H, NOPE_DIM + ROPE_DIM)
    wq_rot = wq[:, :, NOPE_DIM:]
    wq_p = jnp.concatenate([
        wq[:, :, :NOPE_DIM].reshape(q_lora, H * NOPE_DIM),
        _pad_lanes(wq_rot).reshape(q_lora, H * LANES),
        _pad_lanes(_swap_halves(wq_rot)).reshape(q_lora, H * LANES)], axis=1).astype(BF16)
    wkv = w_kv_up.reshape(kv_lora, H, NOPE_DIM + V_DIM)
    wkv_p = jnp.concatenate([wkv[:, :, :NOPE_DIM].reshape(kv_lora, H * NOPE_DIM),
                             wkv[:, :, NOPE_DIM:].reshape(kv_lora, H * V_DIM)], axis=1).astype(BF16)

    def gains(g):
        return jnp.stack([g[:NOPE_DIM], _pad_lanes(g[NOPE_DIM:]), _pad_lanes(_swap_halves(g[NOPE_DIM:]))])

    scale = float((NOPE_DIM + ROPE_DIM) ** -0.5 * np.log2(np.e))
    const = lambda i: (0, 0)
    return pl.pallas_call(
        functools.partial(_mla_prep_kernel, scale=scale),
        grid=(T // bm,),
        in_specs=[
            pl.BlockSpec((bm, q_lora), lambda i: (i, c_cq // q_lora)),
            pl.BlockSpec((bm, kv_lora), lambda i: (i, c_ckv // kv_lora)),
            pl.BlockSpec((bm, LANES), lambda i: (i, c_kr // LANES)),
            pl.BlockSpec((bm, LANES), lambda i: (i, c_krs // LANES)),
            pl.BlockSpec((bm, LANES), lambda i: (i, 0)),
            pl.BlockSpec((bm, LANES), lambda i: (i, 0)),
            pl.BlockSpec((1, q_lora), const),
            pl.BlockSpec((1, kv_lora), const),
            pl.BlockSpec(wq_p.shape, const),
            pl.BlockSpec(wkv_p.shape, const),
            pl.BlockSpec((3, LANES), const),
            pl.BlockSpec((3, LANES), const),
        ],
        out_specs=[
            pl.BlockSpec((H, bm, 2 * LANES), lambda i: (0, i, 0)),
            pl.BlockSpec((H, bm, 2 * LANES), lambda i: (0, i, 0)),
            pl.BlockSpec((H, bm, V_DIM + LANES), lambda i: (0, i, 0)),
        ],
        out_shape=[
            jax.ShapeDtypeStruct((H, T, 2 * LANES), BF16),
            jax.ShapeDtypeStruct((H, T, 2 * LANES), BF16),
            jax.ShapeDtypeStruct((H, T, V_DIM + LANES), BF16),
        ],
        compiler_params=_params("parallel"),
        name="mla_prep",
    )(proj, proj, proj, proj, cs1, cs2, mla_q_norm.reshape(1, -1), mla_kv_norm.reshape(1, -1),
      wq_p, wkv_p, gains(g_q), gains(g_k))


def _flash_kernel(q_ref, k_ref, v_ref, o_ref, m_ref, acc_ref, *, bh, nh):
    i = pl.program_id(2)
    m_ref[...] = jnp.full(m_ref.shape, NEG_BIG, F32)
    acc_ref[...] = jnp.zeros(acc_ref.shape, F32)
    nc = bh // LANES

    def chain(a, j, masked):
        rows = slice(a * bh, (a + 1) * bh)
        start = pl.multiple_of(j * bh, bh)
        k = k_ref[0, pl.ds(start, bh), :]
        v = v_ref[0, pl.ds(start, bh), :]
        s = _dot_nt(q_ref[0, rows, :], k)
        if masked:
            row = lax.broadcasted_iota(jnp.int32, s.shape, 0)
            col = lax.broadcasted_iota(jnp.int32, s.shape, 1)
            s = jnp.where(col <= row, s, NEG_BIG)
        chunks = [s[:, c * LANES:(c + 1) * LANES] for c in range(nc)]
        mx = functools.reduce(jnp.maximum, chunks)
        m_prev = m_ref[rows, :]
        m_new = jnp.maximum(m_prev, jnp.max(mx, axis=-1, keepdims=True))
        alpha = jnp.exp2(m_prev - m_new)
        p = jnp.concatenate([jnp.exp2(c - m_new) for c in chunks], axis=1).astype(BF16)
        alpha2 = jnp.concatenate([alpha] * (acc_ref.shape[1] // LANES), axis=1)
        acc_ref[rows, :] = alpha2 * acc_ref[rows, :] + _dot(p, v)
        m_ref[rows, :] = m_new

    def body(j, carry):
        for a in range(nh):
            chain(a, j, False)
        return carry

    lax.fori_loop(0, i * nh, body, 0)
    for a in range(nh):
        for c in range(a + 1):
            chain(a, i * nh + c, c == a)
    o_ref[...] = (acc_ref[:, 0:V_DIM] / acc_ref[:, V_DIM:2 * V_DIM]).astype(o_ref.dtype)


def _flash_attention(q, k, v, B, S, *, bh=512, nh=4):
    H = q.shape[0]
    bh = min(bh, S // nh)
    bq = bh * nh
    nq = S // bq
    return pl.pallas_call(
        functools.partial(_flash_kernel, bh=bh, nh=nh),
        grid=(B, H, nq),
        in_specs=[
            pl.BlockSpec((1, bq, q.shape[2]), lambda b, h, i: (h, b * nq + i, 0)),
            pl.BlockSpec((1, S, k.shape[2]), lambda b, h, i: (h, b, 0)),
            pl.BlockSpec((1, S, v.shape[2]), lambda b, h, i: (h, b, 0)),
        ],
        out_specs=pl.BlockSpec((bq, V_DIM), lambda b, h, i: (b * nq + i, h)),
        out_shape=jax.ShapeDtypeStruct((B * S, H * V_DIM), BF16),
        scratch_shapes=[pltpu.VMEM((bq, LANES), F32), pltpu.VMEM((bq, v.shape[2]), F32)],
        compiler_params=_params("parallel", "parallel", "arbitrary"),
        name="mla_flash",
    )(q, k, v)


def _log_sigmoid(x):
    return -(jnp.maximum(-x, 0.0) + jnp.log(1.0 + jnp.exp(-jnp.abs(x))))


def _split_dot(a, b_exact):
    hi = a.astype(BF16)
    lo = (a - hi.astype(F32)).astype(BF16)
    return _dot(hi, b_exact) + _dot(lo, b_exact)


def _mlstm_prep_kernel(xm_ref, halo_ref, cw_ref, cb_ref, wq_ref, wk_ref, wv_ref, wif_ref, wift_ref,
                       bif_ref, bift_ref, q_ref, k_ref, v_ref, xc_ref, gc_ref, gr_ref,
                       buf_ref, qkv_ref, *, bm, L):
    i = pl.program_id(1)
    H = MLSTM_HEADS
    D = xm_ref.shape[1]
    DH = D // H
    halo = halo_ref[...].astype(F32)
    buf_ref[0:MLSTM_HALO, :] = jnp.where(i > 0, halo, 0.0)
    xm_bf = xm_ref[...]
    buf_ref[MLSTM_HALO:MLSTM_HALO + bm, :] = xm_bf.astype(F32)
    off = MLSTM_HALO - (MLSTM_CONV_WIDTH - 1)
    acc = jnp.broadcast_to(cb_ref[...], (bm, D))
    for j in range(MLSTM_CONV_WIDTH):
        acc = acc + cw_ref[j:j + 1, :] * buf_ref[off + j:off + j + bm, :]
    xc = _silu(acc)
    xc_bf = xc.astype(BF16)
    xc_ref[...] = xc_bf
    for h in range(H):
        sl = slice(h * DH, (h + 1) * DH)
        qh = _dot(xc_bf[:, sl], wq_ref[h]).astype(BF16)
        k_raw = _dot(xc_bf[:, sl], wk_ref[h])
        vh = _dot(xm_bf[:, sl], wv_ref[h]).astype(BF16)
        q_ref[:, sl] = qh
        k_ref[:, sl] = (k_raw * (DH ** -0.5)).astype(BF16)
        v_ref[:, sl] = vh
        qkv_ref[:, h * DH:(h + 1) * DH] = qh
        qkv_ref[:, D + h * DH:D + (h + 1) * DH] = k_raw.astype(BF16)
        qkv_ref[:, 2 * D + h * DH:2 * D + (h + 1) * DH] = vh
    qkv = qkv_ref[...]
    pre_c = _dot(qkv, wif_ref[...]) + bif_ref[...]
    lane = lax.broadcasted_iota(jnp.int32, pre_c.shape, 1)
    is_f_c = (lane >= H) & (lane < 2 * H)
    lf_c = jnp.where(is_f_c, _log_sigmoid(pre_c), 0.0)
    r = lax.broadcasted_iota(jnp.int32, (bm, bm), 0)
    c = lax.broadcasted_iota(jnp.int32, (bm, bm), 1)
    same = (r // L) == (c // L)
    tri = jnp.where(same & (c <= r), 1.0, 0.0).astype(BF16)
    cum_c = _split_dot_left(tri, lf_c)
    gc_ref[...] = jnp.where(is_f_c, cum_c, pre_c)
    pre_r = _dot_nt(wift_ref[...], qkv) + bift_ref[...]
    row = lax.broadcasted_iota(jnp.int32, pre_r.shape, 0)
    is_f_r = row >= H
    lf_r = jnp.where(is_f_r, _log_sigmoid(pre_r), 0.0)
    trit = jnp.where(same & (r <= c), 1.0, 0.0).astype(BF16)
    cum_r = _split_dot(lf_r, trit)
    gr_ref[...] = jnp.where(is_f_r, cum_r, pre_r)


def _split_dot_left(a_exact, b):
    hi = b.astype(BF16)
    lo = (b - hi.astype(F32)).astype(BF16)
    return _dot(a_exact, hi) + _dot(a_exact, lo)


def _mlstm_prep(proj, col_xm, B, S, conv_w, conv_b, w_mq, w_mk, w_mv, w_if, b_if, *, bm=512, L):
    T = B * S
    H = MLSTM_HEADS
    D = conv_w.shape[1]
    bm = min(bm, S)
    assert bm % L == 0
    nt = S // bm
    cxm = col_xm // D
    hb = bm // MLSTM_HALO
    wif = _pad_lanes(w_if).astype(BF16)
    wift = w_if.T.astype(BF16)
    bif = _pad_lanes(b_if)[None, :]
    bift = b_if[:, None]
    const2 = lambda b_, i: (0, 0)
    const3 = lambda b_, i: (0, 0, 0)
    row_blk = lambda b_, i: (b_ * nt + i, 0)
    act = jax.ShapeDtypeStruct((T, D), BF16)
    return pl.pallas_call(
        functools.partial(_mlstm_prep_kernel, bm=bm, L=L),
        grid=(B, nt),
        in_specs=[
            pl.BlockSpec((bm, D), lambda b_, i: (b_ * nt + i, cxm)),
            pl.BlockSpec((MLSTM_HALO, D), lambda b_, i: (jnp.maximum((b_ * nt + i) * hb - 1, 0), cxm)),
            pl.BlockSpec((MLSTM_CONV_WIDTH, D), const2),
            pl.BlockSpec((1, D), const2),
            pl.BlockSpec(w_mq.shape, const3),
            pl.BlockSpec(w_mk.shape, const3),
            pl.BlockSpec(w_mv.shape, const3),
            pl.BlockSpec(wif.shape, const2),
            pl.BlockSpec(wift.shape, const2),
            pl.BlockSpec((1, LANES), const2),
            pl.BlockSpec((2 * H, 1), const2),
        ],
        out_specs=[
            pl.BlockSpec((bm, D), row_blk), pl.BlockSpec((bm, D), row_blk),
            pl.BlockSpec((bm, D), row_blk), pl.BlockSpec((bm, D), row_blk),
            pl.BlockSpec((bm, LANES), row_blk),
            pl.BlockSpec((2 * H, bm), lambda b_, i: (0, b_ * nt + i)),
        ],
        out_shape=[act, act, act, act,
                   jax.ShapeDtypeStruct((T, LANES), F32),
                   jax.ShapeDtypeStruct((2 * H, T), F32)],
        scratch_shapes=[pltpu.VMEM((MLSTM_HALO + bm, D), F32), pltpu.VMEM((bm, 3 * D), BF16)],
        compiler_params=_params("parallel", "arbitrary"),
        name="mlstm_prep",
    )(proj, proj, conv_w, conv_b.reshape(1, D), w_mq.astype(BF16), w_mk.astype(BF16),
      w_mv.astype(BF16), wif, wift, bif, bift)


def _mlstm_scan_kernel(q_ref, k_ref, v_ref, xc_ref, z_ref, gc_ref, gr_ref, gn_ref, skip_ref,
                       o_ref, c_ref, n_ref, m_ref, *, L):
    H = MLSTM_HEADS
    DH = q_ref.shape[1] // H

    @pl.when(pl.program_id(1) == 0)
    def _():
        c_ref[...] = jnp.zeros(c_ref.shape, F32)
        n_ref[...] = jnp.zeros(n_ref.shape, F32)
        m_ref[...] = jnp.zeros(m_ref.shape, F32)

    row = lax.broadcasted_iota(jnp.int32, (L, L), 0)
    col = lax.broadcasted_iota(jnp.int32, (L, L), 1)
    causal = col <= row
    for h in range(H):
        sl = slice(h * DH, (h + 1) * DH)
        q = q_ref[:, sl]
        k = k_ref[:, sl]
        v = v_ref[:, sl]
        li_c = gc_ref[:, h:h + 1]
        b_c = gc_ref[:, H + h:H + h + 1]
        li_r = gr_ref[h:h + 1, :]
        b_r = gr_ref[H + h:H + h + 1, :]
        m_prev = m_ref[h, 0:1, 0:1]
        C = c_ref[h]
        n = n_ref[h]

        dlog = jnp.where(causal, b_c - b_r + li_r, NEG_BIG)
        inter = b_c + m_prev
        m_comb = jnp.maximum(inter, jnp.max(dlog, axis=-1, keepdims=True))
        w_intra = jnp.exp(dlog - m_comb)
        w_inter = jnp.exp(inter - m_comb)
        s = _dot_nt(q, k) * w_intra
        num = w_inter * _dot(q, C.astype(BF16)) + _dot(s.astype(BF16), v)
        qf = q.astype(F32)
        den = w_inter * jnp.sum(qf * n, axis=-1, keepdims=True) + jnp.sum(s, axis=-1, keepdims=True)
        hh = num / jnp.maximum(jnp.abs(den), jnp.exp(-m_comb))

        b_last = b_c[L - 1:L, :]
        g_c = b_last - b_c + li_c
        m_new = jnp.maximum(b_last + m_prev, jnp.max(g_c, axis=0, keepdims=True))
        wg_c = jnp.exp(g_c - m_new)
        decay = jnp.exp(b_last + m_prev - m_new)
        kw = k.astype(F32) * wg_c
        c_ref[h] = decay * C + _dot(kw.T.astype(BF16), v)
        n_ref[h] = decay * n + jnp.sum(kw, axis=0, keepdims=True)
        m_ref[h] = jnp.broadcast_to(m_new, m_ref.shape[1:])

        mu = jnp.mean(hh, axis=-1, keepdims=True)
        d = hh - mu
        var = jnp.mean(d * d, axis=-1, keepdims=True)
        hn = d * lax.rsqrt(var + EPS) * gn_ref[:, sl]
        hn = hn + skip_ref[:, sl] * xc_ref[:, sl].astype(F32)
        o_ref[:, sl] = (hn * _silu(z_ref[:, sl].astype(F32))).astype(o_ref.dtype)


def _mlstm_scan(q, k, v, xc, proj, col_z, gc, gr, gn_g, skip, B, S, *, L):
    T, D = q.shape
    H = MLSTM_HEADS
    DH = D // H
    nc = S // L
    cz = col_z // D
    row_blk = lambda b_, i: (b_ * nc + i, 0)
    const2 = lambda b_, i: (0, 0)
    return pl.pallas_call(
        functools.partial(_mlstm_scan_kernel, L=L),
        grid=(B, nc),
        in_specs=[
            pl.BlockSpec((L, D), row_blk), pl.BlockSpec((L, D), row_blk),
            pl.BlockSpec((L, D), row_blk), pl.BlockSpec((L, D), row_blk),
            pl.BlockSpec((L, D), lambda b_, i: (b_ * nc + i, cz)),
            pl.BlockSpec((L, LANES), row_blk),
            pl.BlockSpec((2 * H, L), lambda b_, i: (0, b_ * nc + i)),
            pl.BlockSpec((1, D), const2),
            pl.BlockSpec((1, D), const2),
        ],
        out_specs=pl.BlockSpec((L, D), row_blk),
        out_shape=jax.ShapeDtypeStruct((T, D), BF16),
        scratch_shapes=[pltpu.VMEM((H, DH, DH), F32), pltpu.VMEM((H, 1, DH), F32),
                        pltpu.VMEM((H, 8, LANES), F32)],
        compiler_params=_params("parallel", "arbitrary"),
        name="mlstm_scan",
    )(q, k, v, xc, proj, gc, gr, gn_g.reshape(1, D), skip.reshape(1, D))


def _merge_mix_kernel(c_ref, a_ref, m_ref, g0_ref, g1_ref, g2_ref, bg_ref, wc_ref, wa_ref, wm_ref,
                      wmix_ref, x_ref, o_ref):
    N = x_ref.shape[1]

    def gate(t, g_ref):
        return _sigmoid(g_ref[...].astype(F32) + bg_ref[:, t * N:(t + 1) * N])

    y = gate(0, g0_ref) * _dot(c_ref[...], wc_ref[...])
    y = y + gate(1, g1_ref) * _dot(a_ref[...], wa_ref[...])
    y = y + gate(2, g2_ref) * _dot(m_ref[...], wm_ref[...])
    o_ref[...] = x_ref[...] + _dot(y.astype(BF16), wmix_ref[...])


def _layer_spec(w, layer, **kw):
    return pl.BlockSpec((None,) + w.shape[1:], lambda *idx: (layer, 0, 0), **kw)


def _merge_mix(c, a, m, proj, col_gate, b_gate, wc, wa, wm, wmix, layer, x, *, bm=256):
    T, K = c.shape
    N = wc.shape[2]
    bm = min(bm, T)
    assert col_gate % N == 0
    gcb = col_gate // N
    row = lambda i: (i, 0)
    xs = pl.BlockSpec((bm, K), row)
    once = dict(pipeline_mode=pl.Buffered(1))
    return pl.pallas_call(
        _merge_mix_kernel,
        grid=(T // bm,),
        in_specs=[xs, xs, xs,
                  pl.BlockSpec((bm, N), lambda i: (i, gcb)),
                  pl.BlockSpec((bm, N), lambda i: (i, gcb + 1)),
                  pl.BlockSpec((bm, N), lambda i: (i, gcb + 2)),
                  pl.BlockSpec((1, 3 * N), lambda i: (0, 0)),
                  _layer_spec(wc, layer, **once), _layer_spec(wa, layer, **once),
                  _layer_spec(wm, layer, **once), _layer_spec(wmix, layer, **once),
                  pl.BlockSpec((bm, N), row)],
        out_specs=pl.BlockSpec((bm, N), row),
        out_shape=jax.ShapeDtypeStruct((T, N), F32),
        compiler_params=_params("parallel"),
        name="merge_mix",
    )(c, a, m, proj, proj, proj, b_gate.reshape(1, 3 * N), wc, wa, wm, wmix, x)


def _ffn_kernel(x_ref, g_ref, gn_ref, wg_ref, wu_ref, wo_ref, o_ref, h_ref, *, emit_next):
    j = pl.program_id(1)

    @pl.when(j == 0)
    def _():
        xf = x_ref[...]
        h_ref[...] = (_rms(xf, xf.shape[-1]) * g_ref[...]).astype(BF16)
        o_ref[...] = xf

    h = h_ref[...]
    y = (_silu(_dot(h, wg_ref[...])) * _dot(h, wu_ref[...])).astype(BF16)
    o_ref[...] += _dot(y, wo_ref[...])

    if emit_next:
        @pl.when(j == pl.num_programs(1) - 1)
        def _():
            of = o_ref[...]
            h_ref[...] = (_rms(of, of.shape[-1]) * gn_ref[...]).astype(BF16)


def _ffn(x, gain, w_in, w_out, layer, next_gain=None, *, bm=1024, bh=512):
    T, D = x.shape
    F = w_out.shape[1]
    bm, bh = min(bm, T), min(bh, F)
    nh = F // bh
    emit = next_gain is not None
    gn = (next_gain if emit else gain).reshape(1, D).astype(F32)
    row = pl.BlockSpec((bm, D), lambda i, j: (i, 0))
    vec = pl.BlockSpec((1, D), lambda i, j: (0, 0))
    out_shape = [jax.ShapeDtypeStruct((T, D), F32)]
    out_specs = [row]
    scratch = [pltpu.VMEM((bm, D), BF16)]
    if emit:
        out_shape.append(jax.ShapeDtypeStruct((T, D), BF16))
        out_specs.append(row)
        scratch = []
    res = pl.pallas_call(
        functools.partial(_ffn_kernel, emit_next=emit),
        grid=(T // bm, nh),
        in_specs=[pl.BlockSpec((bm, D), lambda i, j: (i, 0), pipeline_mode=pl.Buffered(1)), vec, vec,
                  pl.BlockSpec((None, D, bh), lambda i, j: (layer, 0, j)),
                  pl.BlockSpec((None, D, bh), lambda i, j: (layer, 0, j + nh)),
                  pl.BlockSpec((None, bh, D), lambda i, j: (layer, j, 0))],
        out_specs=out_specs,
        out_shape=out_shape,
        scratch_shapes=scratch,
        compiler_params=_params("parallel", "arbitrary"),
        name="ffn",
    )(x, gain.reshape(1, D).astype(F32), gn, w_in, w_in, w_out)
    return (res[0], res[1]) if emit else (res[0], None)


def _norm_kernel(x_ref, g_ref, o_ref):
    xf = x_ref[...]
    o_ref[...] = (_rms(xf, xf.shape[-1]) * g_ref[...]).astype(o_ref.dtype)


def _rmsnorm(x, gain, *, bm=1024):
    T, D = x.shape
    bm = min(bm, T)
    return pl.pallas_call(
        _norm_kernel,
        grid=(T // bm,),
        in_specs=[pl.BlockSpec((bm, D), lambda i: (i, 0)), pl.BlockSpec((1, D), lambda i: (0, 0))],
        out_specs=pl.BlockSpec((bm, D), lambda i: (i, 0)),
        out_shape=jax.ShapeDtypeStruct((T, D), BF16),
        compiler_params=_params("parallel"),
        name="rmsnorm",
    )(x, gain.reshape(1, D).astype(F32))


def _mem_kv_kernel(mem_ref, g_ref, w_ref, gk_ref, k_ref, v_ref):
    H, DH = XATTN_HEADS, XATTN_HEAD_DIM
    x = mem_ref[...]
    h = (_rms(x, x.shape[-1]) * g_ref[...]).astype(BF16)
    kv = _dot(h, w_ref[...])
    for hd in range(H):
        kh = kv[:, hd * DH:(hd + 1) * DH]
        k_ref[:, hd * DH:(hd + 1) * DH] = (_rms(kh, DH) * gk_ref[...]).astype(BF16)
    v_ref[...] = kv[:, H * DH:].astype(BF16)


def _mem_kv(mem2d, norm_mem, w_xkv, g_k, *, bm=256):
    M, D = mem2d.shape
    HD = XATTN_HEADS * XATTN_HEAD_DIM
    bm = min(bm, M)
    const = lambda i: (0, 0)
    out = jax.ShapeDtypeStruct((M, HD), BF16)
    return pl.pallas_call(
        _mem_kv_kernel,
        grid=(M // bm,),
        in_specs=[pl.BlockSpec((bm, D), lambda i: (i, 0)), pl.BlockSpec((1, D), const),
                  pl.BlockSpec((D, 2 * HD), const), pl.BlockSpec((1, XATTN_HEAD_DIM), const)],
        out_specs=[pl.BlockSpec((bm, HD), lambda i: (i, 0)), pl.BlockSpec((bm, HD), lambda i: (i, 0))],
        out_shape=[out, out],
        compiler_params=_params("parallel"),
        name="xattn_mem_kv",
    )(mem2d, norm_mem.reshape(1, D), w_xkv.astype(BF16), g_k.reshape(1, -1))


def _xattn_kernel(x_ref, g_ref, wq_ref, gq_ref, k_ref, v_ref, wo_ref, o_ref, att_ref, *, scale):
    H, DH = XATTN_HEADS, XATTN_HEAD_DIM
    x = x_ref[...]
    h = (_rms(x, x.shape[-1]) * g_ref[...]).astype(BF16)
    q = _dot(h, wq_ref[...])
    for hd in range(H):
        sl = slice(hd * DH, (hd + 1) * DH)
        qh = (_rms(q[:, sl], DH) * gq_ref[...] * scale).astype(BF16)
        s = _dot_nt(qh, k_ref[:, sl])
        e = jnp.exp(s - jnp.max(s, axis=-1, keepdims=True))
        o = _dot(e.astype(BF16), v_ref[:, sl]) / jnp.sum(e, axis=-1, keepdims=True)
        att_ref[:, sl] = o.astype(BF16)
    o_ref[...] = x + _dot(att_ref[...], wo_ref[...])


def _xattn(x, norm_x, w_xq, g_q, k, v, w_xo, B, S, *, bm=512):
    T, D = x.shape
    HD = XATTN_HEADS * XATTN_HEAD_DIM
    M = k.shape[0] // B
    bm = min(bm, S)
    nt = S // bm
    const = lambda b_, i: (0, 0)
    return pl.pallas_call(
        functools.partial(_xattn_kernel, scale=float(XATTN_HEAD_DIM ** -0.5)),
        grid=(B, nt),
        in_specs=[
            pl.BlockSpec((bm, D), lambda b_, i: (b_ * nt + i, 0)),
            pl.BlockSpec((1, D), const),
            pl.BlockSpec((D, HD), const),
            pl.BlockSpec((1, XATTN_HEAD_DIM), const),
            pl.BlockSpec((M, HD), lambda b_, i: (b_, 0)),
            pl.BlockSpec((M, HD), lambda b_, i: (b_, 0)),
            pl.BlockSpec((HD, D), const),
        ],
        out_specs=pl.BlockSpec((bm, D), lambda b_, i: (b_ * nt + i, 0)),
        out_shape=jax.ShapeDtypeStruct((T, D), F32),
        scratch_shapes=[pltpu.VMEM((bm, HD), BF16)],
        compiler_params=_params("parallel", "arbitrary"),
        name="xattn",
    )(x, norm_x.reshape(1, D), w_xq.astype(BF16), g_q.reshape(1, -1), k, v, w_xo.astype(BF16))


def _layout_w_in(w_in, C, q_lora, kv_lora, D_ml):
    n_head = 2 * C + q_lora + kv_lora + ROPE_DIM
    tail = w_in[..., n_head:].astype(BF16)
    kr = w_in[..., n_head - ROPE_DIM:n_head].astype(BF16)
    half = ROPE_DIM // 2
    zpad = jnp.zeros(kr.shape[:-1] + (LANES - ROPE_DIM,), BF16)
    head = jnp.concatenate([w_in[..., :n_head].astype(BF16), zpad, kr[..., half:], kr[..., :half], zpad],
                           axis=-1)
    o_kr = n_head - ROPE_DIM
    cols = dict(a=0, g=C, cq=2 * C, ckv=2 * C + q_lora, kr=o_kr, krs=o_kr + LANES,
                xm=0, z=D_ml, gate=2 * D_ml)
    return head, tail, cols


def kernel(x, mem, positions, norm_mix, w_in, b_gate, conv_dw, conv_dw_b, conv_ln_g, conv_ln_b, w_conv_out, mla_q_norm, mla_kv_norm, w_q_up, w_kv_up, mla_g_q, mla_g_k, w_mla_out, mlstm_conv_w, mlstm_conv_b, w_mq, w_mk, w_mv, w_if, b_if, mlstm_gn_g, mlstm_skip, w_mlstm_out, w_mix_out, norm_x, norm_mem, w_xq, w_xkv, xattn_g_q, xattn_g_k, w_xo, norm_ffn, w_ffn_in, w_ffn_out):
    B, S, D = x.shape
    T = B * S
    depth = w_in.shape[0]
    C = conv_dw.shape[2]
    D_ml = mlstm_conv_w.shape[2]
    xt = x.reshape(T, D)
    mem2d = mem.reshape(-1, D)
    cs1, cs2 = _rope_tables(positions.reshape(T, 1))
    w_head, w_tail, cols = _layout_w_in(w_in, C, w_q_up.shape[1], w_kv_up.shape[1], D_ml)
    w_conv_out, w_mla_out, w_mlstm_out, w_mix_out, w_ffn_in, w_ffn_out = (
        w.astype(BF16) for w in (w_conv_out, w_mla_out, w_mlstm_out, w_mix_out, w_ffn_in, w_ffn_out))
    h = _rmsnorm(xt, norm_mix[0])
    for l in range(depth):
        L = min(MLSTM_CHUNK if l == 0 else MLSTM_CHUNK * 2, S)
        ph = _matmul(h, w_head, l, bm=2048, bn=1024, name="in_proj_head")
        pt = _matmul(h, w_tail, l, bm=2048, bn=1024, name="in_proj_tail")
        c = _conv_branch(ph, cols["a"], cols["g"], C, B, S, conv_dw[l], conv_dw_b[l],
                         conv_ln_g[l], conv_ln_b[l])
        q, k, v = _mla_prep(ph, (cols["cq"], cols["ckv"], cols["kr"], cols["krs"]), cs1, cs2,
                            mla_q_norm[l], mla_kv_norm[l], w_q_up[l], w_kv_up[l], mla_g_q[l], mla_g_k[l])
        a = _flash_attention(q, k, v, B, S)
        mq, mk, mv, xc, gc, gr = _mlstm_prep(pt, cols["xm"], B, S, mlstm_conv_w[l], mlstm_conv_b[l],
                                             w_mq[l], w_mk[l], w_mv[l], w_if[l], b_if[l], L=L)
        m = _mlstm_scan(mq, mk, mv, xc, pt, cols["z"], gc, gr, mlstm_gn_g[l], mlstm_skip[l], B, S, L=L)
        xt = _merge_mix(c, a, m, pt, cols["gate"], b_gate[l], w_conv_out, w_mla_out,
                        w_mlstm_out, w_mix_out, l, xt)
        xk, xv = _mem_kv(mem2d, norm_mem[l], w_xkv[l], xattn_g_k[l])
        xt = _xattn(xt, norm_x[l], w_xq[l], xattn_g_q[l], xk, xv, w_xo[l], B, S)
        xt, h = _ffn(xt, norm_ffn[l], w_ffn_in, w_ffn_out, l,
                     next_gain=norm_mix[l + 1] if l + 1 < depth else None)
    return xt.reshape(B, S, D)
```

```python
import functools

import jax
import jax.numpy as jnp
import numpy as np
from jax import lax
from jax.experimental import pallas as pl
from jax.experimental.pallas import tpu as pltpu

F32 = jnp.float32
BF16 = jnp.bfloat16

EPS = 1e-6
CONV_WIDTH = 31
MLA_HEADS = 8
NOPE_DIM = 128
ROPE_DIM = 64
V_DIM = 128
ROPE_BASE = 10000.0
MLSTM_HEADS = 4
MLSTM_CONV_WIDTH = 4
XATTN_HEADS = 4
XATTN_HEAD_DIM = 128
LANES = 128
SUBLANES = 8
NEG_BIG = -1e30
VMEM_LIMIT = 60 * 1024 * 1024

MLSTM_CHUNK = 512
CONV_HALO = 128
CONV_LOOKBACK = 32
MLSTM_HALO = 16


def _params(*sem):
    return pltpu.CompilerParams(dimension_semantics=sem, vmem_limit_bytes=VMEM_LIMIT)


def _sigmoid(x):
    return 0.5 * jnp.tanh(0.5 * x) + 0.5


def _silu(x):
    return x * _sigmoid(x)


def _dot(a, b):
    return jnp.dot(a, b, preferred_element_type=F32)


def _dot_nt(a, b):
    return lax.dot_general(a, b, (((1,), (1,)), ((), ())), preferred_element_type=F32)


def _rms(x, n):
    ms = jnp.sum(x * x, axis=-1, keepdims=True) * (1.0 / n)
    return x * lax.rsqrt(ms + EPS)


def _mm_kernel(x_ref, w_ref, o_ref):
    o_ref[...] = _dot(x_ref[...], w_ref[...]).astype(o_ref.dtype)


def _matmul(x, w, layer, *, bm, bn, name):
    M, K = x.shape
    N = w.shape[2]
    bm, bn = min(bm, M), min(bn, N)
    assert M % bm == 0 and N % bn == 0 and w.shape[1] == K
    return pl.pallas_call(
        _mm_kernel,
        grid=(M // bm, N // bn),
        in_specs=[pl.BlockSpec((bm, K), lambda i, j: (i, 0)),
                  pl.BlockSpec((None, K, bn), lambda i, j: (layer, 0, j))],
        out_specs=pl.BlockSpec((bm, bn), lambda i, j: (i, j)),
        out_shape=jax.ShapeDtypeStruct((M, N), BF16),
        compiler_params=_params("parallel", "arbitrary"),
        name=name,
    )(x, w)


def _conv_kernel(a_ref, g_ref, p_ref, w_ref, b_ref, lg_ref, lb_ref, o_ref, u_ref, *, bt, rt):
    i = pl.program_id(1)

    @pl.when(i == 0)
    def _():
        u_ref[0:CONV_HALO, :] = jnp.zeros((CONV_HALO, u_ref.shape[1]), BF16)

    @pl.when(i > 0)
    def _():
        u_ref[0:CONV_HALO, :] = u_ref[bt:bt + CONV_HALO, :]

    a = a_ref[...].astype(F32)
    g = g_ref[...].astype(F32)
    u_ref[CONV_HALO:CONV_HALO + bt, :] = (a * _sigmoid(g)).astype(BF16)
    bias = b_ref[...]
    sb = p_ref.shape[1] - CONV_LOOKBACK
    for k0 in range(0, bt, sb):
        u_win = u_ref[k0:k0 + sb + CONV_HALO, :]
        sh = [u_win[CONV_HALO - CONV_LOOKBACK:, :].astype(F32)]
        sh += [_dot(p_ref[r - 1], u_win) for r in range(1, SUBLANES)]
        for r0 in range(0, sb, rt):
            acc = [jnp.broadcast_to(bias, (SUBLANES, bias.shape[1]))] * (rt // SUBLANES)
            for j in range(CONV_WIDTH):
                shift = CONV_WIDTH - 1 - j
                r, start = shift % SUBLANES, r0 + CONV_LOOKBACK - (shift // SUBLANES) * SUBLANES
                wj = w_ref[j]
                acc = [acc[q] + wj * sh[r][start + q * SUBLANES:start + (q + 1) * SUBLANES, :]
                       for q in range(rt // SUBLANES)]
            acc = jnp.concatenate(acc, axis=0)
            mu = jnp.mean(acc, axis=-1, keepdims=True)
            d = acc - mu
            var = jnp.mean(d * d, axis=-1, keepdims=True)
            y = d * lax.rsqrt(var + EPS) * lg_ref[...] + lb_ref[...]
            o_ref[k0 + r0:k0 + r0 + rt, :] = _silu(y).astype(o_ref.dtype)


def _shift_matrices(sb):
    m = np.arange(sb + CONV_LOOKBACK)[None, :, None]
    k = np.arange(sb + CONV_HALO)[None, None, :]
    r = np.arange(1, SUBLANES)[:, None, None]
    return jnp.asarray(k == CONV_HALO - CONV_LOOKBACK + m - r, dtype=BF16)


def _conv_branch(proj, col_a, col_g, C, B, S, w, b, ln_g, ln_b, *, bt=256, sb=128, rt=32):
    bt = min(bt, S)
    assert bt >= CONV_HALO and bt % sb == 0 and sb % rt == 0
    nt = S // bt
    ca, cg = col_a // C, col_g // C
    pmat = _shift_matrices(sb)
    w = jnp.broadcast_to(w[:, None, :], (CONV_WIDTH, SUBLANES, C))
    const = lambda b_, i: (0, 0)
    return pl.pallas_call(
        functools.partial(_conv_kernel, bt=bt, rt=rt),
        grid=(B, nt),
        in_specs=[
            pl.BlockSpec((bt, C), lambda b_, i: (b_ * nt + i, ca)),
            pl.BlockSpec((bt, C), lambda b_, i: (b_ * nt + i, cg)),
            pl.BlockSpec(pmat.shape, lambda b_, i: (0, 0, 0)),
            pl.BlockSpec((CONV_WIDTH, SUBLANES, C), lambda b_, i: (0, 0, 0)),
            pl.BlockSpec((1, C), const),
            pl.BlockSpec((1, C), const),
            pl.BlockSpec((1, C), const),
        ],
        out_specs=pl.BlockSpec((bt, C), lambda b_, i: (b_ * nt + i, 0)),
        out_shape=jax.ShapeDtypeStruct((B * S, C), BF16),
        scratch_shapes=[pltpu.VMEM((CONV_HALO + bt, C), BF16)],
        compiler_params=_params("parallel", "arbitrary"),
        name="conformer_conv",
    )(proj, proj, pmat, w, b.reshape(1, C), ln_g.reshape(1, C), ln_b.reshape(1, C))


def _rope_kernel(pos_ref, freq_ref, sgn_ref, cs1_ref, cs2_ref):
    ang = pos_ref[...].astype(F32) * freq_ref[...]
    cs1_ref[...] = jnp.cos(ang)
    cs2_ref[...] = jnp.sin(ang) * sgn_ref[...]


def _rope_tables(pos, *, bm=1024):
    T = pos.shape[0]
    bm = min(bm, T)
    half = ROPE_DIM // 2
    inv_freq = ROPE_BASE ** (-jnp.arange(half, dtype=F32) / half)
    freq = _pad_lanes(jnp.concatenate([inv_freq, inv_freq]))[None, :]
    sgn = _pad_lanes(jnp.concatenate([-jnp.ones((half,), F32), jnp.ones((half,), F32)]))[None, :]
    const = lambda i: (0, 0)
    tab = jax.ShapeDtypeStruct((T, LANES), F32)
    return pl.pallas_call(
        _rope_kernel,
        grid=(T // bm,),
        in_specs=[pl.BlockSpec((bm, 1), lambda i: (i, 0)), pl.BlockSpec((1, LANES), const),
                  pl.BlockSpec((1, LANES), const)],
        out_specs=[pl.BlockSpec((bm, LANES), lambda i: (i, 0)), pl.BlockSpec((bm, LANES), lambda i: (i, 0))],
        out_shape=[tab, tab],
        compiler_params=_params("parallel"),
        name="rope_tables",
    )(pos, freq, sgn)


def _mla_prep_kernel(cq_ref, ckv_ref, kr_ref, krs_ref, cs1_ref, cs2_ref,
                     qn_ref, kvn_ref, wq_ref, wkv_ref, gq_ref, gk_ref,
                     q_ref, k_ref, v_ref, *, scale):
    H = MLA_HEADS
    cs1 = cs1_ref[...]
    cs2 = cs2_ref[...]

    cq = cq_ref[...].astype(F32)
    hq = (_rms(cq, cq.shape[-1]) * qn_ref[...]).astype(BF16)
    qall = _dot(hq, wq_ref[...])
    ckv = ckv_ref[...].astype(F32)
    hkv = (_rms(ckv, ckv.shape[-1]) * kvn_ref[...]).astype(BF16)
    kvall = _dot(hkv, wkv_ref[...])

    g_nope_q, g_rot_q, g_rots_q = gq_ref[0:1, :], gq_ref[1:2, :], gq_ref[2:3, :]
    g_nope_k, g_rot_k, g_rots_k = gk_ref[0:1, :], gk_ref[1:2, :], gk_ref[2:3, :]

    kr = kr_ref[...].astype(F32)
    krs = krs_ref[...].astype(F32)
    rs = lax.rsqrt(jnp.sum(kr * kr, axis=-1, keepdims=True) * (1.0 / ROPE_DIM) + EPS)
    k_rot = ((kr * rs * g_rot_k) * cs1 + (krs * rs * g_rots_k) * cs2).astype(BF16)

    for h in range(H):
        qn = qall[:, h * LANES:(h + 1) * LANES]
        q_ref[h, :, 0:LANES] = (_rms(qn, NOPE_DIM) * g_nope_q * scale).astype(BF16)
        qr = qall[:, (H + h) * LANES:(H + h + 1) * LANES]
        qs = qall[:, (2 * H + h) * LANES:(2 * H + h + 1) * LANES]
        rq = lax.rsqrt(jnp.sum(qr * qr, axis=-1, keepdims=True) * (1.0 / ROPE_DIM) + EPS)
        q_rot = (qr * rq * g_rot_q) * cs1 + (qs * rq * g_rots_q) * cs2
        q_ref[h, :, LANES:2 * LANES] = (q_rot * scale).astype(BF16)
        kn = kvall[:, h * LANES:(h + 1) * LANES]
        k_ref[h, :, 0:LANES] = (_rms(kn, NOPE_DIM) * g_nope_k).astype(BF16)
        k_ref[h, :, LANES:2 * LANES] = k_rot
        v_ref[h, :, 0:V_DIM] = kvall[:, (H + h) * LANES:(H + h + 1) * LANES].astype(BF16)
        v_ref[h, :, V_DIM:V_DIM + LANES] = jnp.ones((kvall.shape[0], LANES), BF16)


def _pad_lanes(v, n=LANES):
    return jnp.pad(v, [(0, 0)] * (v.ndim - 1) + [(0, n - v.shape[-1])])


def _swap_halves(v):
    half = v.shape[-1] // 2
    return jnp.concatenate([v[..., half:], v[..., :half]], axis=-1)


def _mla_prep(proj, cols, cs1, cs2, mla_q_norm, mla_kv_norm, w_q_up, w_kv_up, g_q, g_k, *, bm=512):
    T = proj.shape[0]
    bm = min(bm, T)
    H = MLA_HEADS
    q_lora, kv_lora = w_q_up.shape[0], w_kv_up.shape[0]
    c_cq, c_ckv, c_kr, c_krs = cols
    wq = w_q_up.reshape(q_lora, H, NOPE_DIM + ROPE_DIM)
    wq_rot = wq[:, :, NOPE_DIM:]
    wq_p = jnp.concatenate([
        wq[:, :, :NOPE_DIM].reshape(q_lora, H * NOPE_DIM),
        _pad_lanes(wq_rot).reshape(q_lora, H * LANES),
        _pad_lanes(_swap_halves(wq_rot)).reshape(q_lora, H * LANES)], axis=1).astype(BF16)
    wkv = w_kv_up.reshape(kv_lora, H, NOPE_DIM + V_DIM)
    wkv_p = jnp.concatenate([wkv[:, :, :NOPE_DIM].reshape(kv_lora, H * NOPE_DIM),
                             wkv[:, :, NOPE_DIM:].reshape(kv_lora, H * V_DIM)], axis=1).astype(BF16)

    def gains(g):
        return jnp.stack([g[:NOPE_DIM], _pad_lanes(g[NOPE_DIM:]), _pad_lanes(_swap_halves(g[NOPE_DIM:]))])

    scale = float((NOPE_DIM + ROPE_DIM) ** -0.5 * np.log2(np.e))
    const = lambda i: (0, 0)
    return pl.pallas_call(
        functools.partial(_mla_prep_kernel, scale=scale),
        grid=(T // bm,),
        in_specs=[
            pl.BlockSpec((bm, q_lora), lambda i: (i, c_cq // q_lora)),
            pl.BlockSpec((bm, kv_lora), lambda i: (i, c_ckv // kv_lora)),
            pl.BlockSpec((bm, LANES), lambda i: (i, c_kr // LANES)),
            pl.BlockSpec((bm, LANES), lambda i: (i, c_krs // LANES)),
            pl.BlockSpec((bm, LANES), lambda i: (i, 0)),
            pl.BlockSpec((bm, LANES), lambda i: (i, 0)),
            pl.BlockSpec((1, q_lora), const),
            pl.BlockSpec((1, kv_lora), const),
            pl.BlockSpec(wq_p.shape, const),
            pl.BlockSpec(wkv_p.shape, const),
            pl.BlockSpec((3, LANES), const),
            pl.BlockSpec((3, LANES), const),
        ],
        out_specs=[
            pl.BlockSpec((H, bm, 2 * LANES), lambda i: (0, i, 0)),
            pl.BlockSpec((H, bm, 2 * LANES), lambda i: (0, i, 0)),
            pl.BlockSpec((H, bm, V_DIM + LANES), lambda i: (0, i, 0)),
        ],
        out_shape=[
            jax.ShapeDtypeStruct((H, T, 2 * LANES), BF16),
            jax.ShapeDtypeStruct((H, T, 2 * LANES), BF16),
            jax.ShapeDtypeStruct((H, T, V_DIM + LANES), BF16),
        ],
        compiler_params=_params("parallel"),
        name="mla_prep",
    )(proj, proj, proj, proj, cs1, cs2, mla_q_norm.reshape(1, -1), mla_kv_norm.reshape(1, -1),
      wq_p, wkv_p, gains(g_q), gains(g_k))


def _flash_kernel(q_ref, k_ref, v_ref, o_ref, m_ref, acc_ref, *, bh, nh):
    i = pl.program_id(2)
    m_ref[...] = jnp.full(m_ref.shape, NEG_BIG, F32)
    acc_ref[...] = jnp.zeros(acc_ref.shape, F32)
    nc = bh // LANES

    def chain(a, j, masked):
        rows = slice(a * bh, (a + 1) * bh)
        start = pl.multiple_of(j * bh, bh)
        k = k_ref[0, pl.ds(start, bh), :]
        v = v_ref[0, pl.ds(start, bh), :]
        s = _dot_nt(q_ref[0, rows, :], k)
        if masked:
            row = lax.broadcasted_iota(jnp.int32, s.shape, 0)
            col = lax.broadcasted_iota(jnp.int32, s.shape, 1)
            s = jnp.where(col <= row, s, NEG_BIG)
        chunks = [s[:, c * LANES:(c + 1) * LANES] for c in range(nc)]
        mx = functools.reduce(jnp.maximum, chunks)
        m_prev = m_ref[rows, :]
        m_new = jnp.maximum(m_prev, jnp.max(mx, axis=-1, keepdims=True))
        alpha = jnp.exp2(m_prev - m_new)
        p = jnp.concatenate([jnp.exp2(c - m_new) for c in chunks], axis=1).astype(BF16)
        alpha2 = jnp.concatenate([alpha] * (acc_ref.shape[1] // LANES), axis=1)
        acc_ref[rows, :] = alpha2 * acc_ref[rows, :] + _dot(p, v)
        m_ref[rows, :] = m_new

    def body(j, carry):
        for a in range(nh):
            chain(a, j, False)
        return carry

    lax.fori_loop(0, i * nh, body, 0)
    for a in range(nh):
        for c in range(a + 1):
            chain(a, i * nh + c, c == a)
    o_ref[...] = (acc_ref[:, 0:V_DIM] / acc_ref[:, V_DIM:2 * V_DIM]).astype(o_ref.dtype)


def _flash_attention(q, k, v, B, S, *, bh=512, nh=4):
    H = q.shape[0]
    bh = min(bh, S // nh)
    bq = bh * nh
    nq = S // bq
    return pl.pallas_call(
        functools.partial(_flash_kernel, bh=bh, nh=nh),
        grid=(B, H, nq),
        in_specs=[
            pl.BlockSpec((1, bq, q.shape[2]), lambda b, h, i: (h, b * nq + i, 0)),
            pl.BlockSpec((1, S, k.shape[2]), lambda b, h, i: (h, b, 0)),
            pl.BlockSpec((1, S, v.shape[2]), lambda b, h, i: (h, b, 0)),
        ],
        out_specs=pl.BlockSpec((bq, V_DIM), lambda b, h, i: (b * nq + i, h)),
        out_shape=jax.ShapeDtypeStruct((B * S, H * V_DIM), BF16),
        scratch_shapes=[pltpu.VMEM((bq, LANES), F32), pltpu.VMEM((bq, v.shape[2]), F32)],
        compiler_params=_params("parallel", "parallel", "arbitrary"),
        name="mla_flash",
    )(q, k, v)


def _log_sigmoid(x):
    return -(jnp.maximum(-x, 0.0) + jnp.log(1.0 + jnp.exp(-jnp.abs(x))))


def _split_dot(a, b_exact):
    hi = a.astype(BF16)
    lo = (a - hi.astype(F32)).astype(BF16)
    return _dot(hi, b_exact) + _dot(lo, b_exact)


def _mlstm_prep_kernel(xm_ref, halo_ref, cw_ref, cb_ref, wq_ref, wk_ref, wv_ref, wif_ref, wift_ref,
                       bif_ref, bift_ref, q_ref, k_ref, v_ref, xc_ref, gc_ref, gr_ref,
                       buf_ref, qkv_ref, *, bm, L):
    i = pl.program_id(1)
    H = MLSTM_HEADS
    D = xm_ref.shape[1]
    DH = D // H
    halo = halo_ref[...].astype(F32)
    buf_ref[0:MLSTM_HALO, :] = jnp.where(i > 0, halo, 0.0)
    xm_bf = xm_ref[...]
    buf_ref[MLSTM_HALO:MLSTM_HALO + bm, :] = xm_bf.astype(F32)
    off = MLSTM_HALO - (MLSTM_CONV_WIDTH - 1)
    acc = jnp.broadcast_to(cb_ref[...], (bm, D))
    for j in range(MLSTM_CONV_WIDTH):
        acc = acc + cw_ref[j:j + 1, :] * buf_ref[off + j:off + j + bm, :]
    xc = _silu(acc)
    xc_bf = xc.astype(BF16)
    xc_ref[...] = xc_bf
    for h in range(H):
        sl = slice(h * DH, (h + 1) * DH)
        qh = _dot(xc_bf[:, sl], wq_ref[h]).astype(BF16)
        k_raw = _dot(xc_bf[:, sl], wk_ref[h])
        vh = _dot(xm_bf[:, sl], wv_ref[h]).astype(BF16)
        q_ref[:, sl] = qh
        k_ref[:, sl] = (k_raw * (DH ** -0.5)).astype(BF16)
        v_ref[:, sl] = vh
        qkv_ref[:, h * DH:(h + 1) * DH] = qh
        qkv_ref[:, D + h * DH:D + (h + 1) * DH] = k_raw.astype(BF16)
        qkv_ref[:, 2 * D + h * DH:2 * D + (h + 1) * DH] = vh
    qkv = qkv_ref[...]
    pre_c = _dot(qkv, wif_ref[...]) + bif_ref[...]
    lane = lax.broadcasted_iota(jnp.int32, pre_c.shape, 1)
    is_f_c = (lane >= H) & (lane < 2 * H)
    lf_c = jnp.where(is_f_c, _log_sigmoid(pre_c), 0.0)
    r = lax.broadcasted_iota(jnp.int32, (bm, bm), 0)
    c = lax.broadcasted_iota(jnp.int32, (bm, bm), 1)
    same = (r // L) == (c // L)
    tri = jnp.where(same & (c <= r), 1.0, 0.0).astype(BF16)
    cum_c = _split_dot_left(tri, lf_c)
    gc_ref[...] = jnp.where(is_f_c, cum_c, pre_c)
    pre_r = _dot_nt(wift_ref[...], qkv) + bift_ref[...]
    row = lax.broadcasted_iota(jnp.int32, pre_r.shape, 0)
    is_f_r = row >= H
    lf_r = jnp.where(is_f_r, _log_sigmoid(pre_r), 0.0)
    trit = jnp.where(same & (r <= c), 1.0, 0.0).astype(BF16)
    cum_r = _split_dot(lf_r, trit)
    gr_ref[...] = jnp.where(is_f_r, cum_r, pre_r)


def _split_dot_left(a_exact, b):
    hi = b.astype(BF16)
    lo = (b - hi.astype(F32)).astype(BF16)
    return _dot(a_exact, hi) + _dot(a_exact, lo)


def _mlstm_prep(proj, col_xm, B, S, conv_w, conv_b, w_mq, w_mk, w_mv, w_if, b_if, *, bm=512, L):
    T = B * S
    H = MLSTM_HEADS
    D = conv_w.shape[1]
    bm = min(bm, S)
    assert bm % L == 0
    nt = S // bm
    cxm = col_xm // D
    hb = bm // MLSTM_HALO
    wif = _pad_lanes(w_if).astype(BF16)
    wift = w_if.T.astype(BF16)
    bif = _pad_lanes(b_if)[None, :]
    bift = b_if[:, None]
    const2 = lambda b_, i: (0, 0)
    const3 = lambda b_, i: (0, 0, 0)
    row_blk = lambda b_, i: (b_ * nt + i, 0)
    act = jax.ShapeDtypeStruct((T, D), BF16)
    return pl.pallas_call(
        functools.partial(_mlstm_prep_kernel, bm=bm, L=L),
        grid=(B, nt),
        in_specs=[
            pl.BlockSpec((bm, D), lambda b_, i: (b_ * nt + i, cxm)),
            pl.BlockSpec((MLSTM_HALO, D), lambda b_, i: (jnp.maximum((b_ * nt + i) * hb - 1, 0), cxm)),
            pl.BlockSpec((MLSTM_CONV_WIDTH, D), const2),
            pl.BlockSpec((1, D), const2),
            pl.BlockSpec(w_mq.shape, const3),
            pl.BlockSpec(w_mk.shape, const3),
            pl.BlockSpec(w_mv.shape, const3),
            pl.BlockSpec(wif.shape, const2),
            pl.BlockSpec(wift.shape, const2),
            pl.BlockSpec((1, LANES), const2),
            pl.BlockSpec((2 * H, 1), const2),
        ],
        out_specs=[
            pl.BlockSpec((bm, D), row_blk), pl.BlockSpec((bm, D), row_blk),
            pl.BlockSpec((bm, D), row_blk), pl.BlockSpec((bm, D), row_blk),
            pl.BlockSpec((bm, LANES), row_blk),
            pl.BlockSpec((2 * H, bm), lambda b_, i: (0, b_ * nt + i)),
        ],
        out_shape=[act, act, act, act,
                   jax.ShapeDtypeStruct((T, LANES), F32),
                   jax.ShapeDtypeStruct((2 * H, T), F32)],
        scratch_shapes=[pltpu.VMEM((MLSTM_HALO + bm, D), F32), pltpu.VMEM((bm, 3 * D), BF16)],
        compiler_params=_params("parallel", "arbitrary"),
        name="mlstm_prep",
    )(proj, proj, conv_w, conv_b.reshape(1, D), w_mq.astype(BF16), w_mk.astype(BF16),
      w_mv.astype(BF16), wif, wift, bif, bift)


def _mlstm_scan_kernel(q_ref, k_ref, v_ref, xc_ref, z_ref, gc_ref, gr_ref, gn_ref, skip_ref,
                       o_ref, c_ref, n_ref, m_ref, *, L):
    H = MLSTM_HEADS
    DH = q_ref.shape[1] // H

    @pl.when(pl.program_id(1) == 0)
    def _():
        c_ref[...] = jnp.zeros(c_ref.shape, F32)
        n_ref[...] = jnp.zeros(n_ref.shape, F32)
        m_ref[...] = jnp.zeros(m_ref.shape, F32)

    row = lax.broadcasted_iota(jnp.int32, (L, L), 0)
    col = lax.broadcasted_iota(jnp.int32, (L, L), 1)
    causal = col <= row
    for h in range(H):
        sl = slice(h * DH, (h + 1) * DH)
        q = q_ref[:, sl]
        k = k_ref[:, sl]
        v = v_ref[:, sl]
        li_c = gc_ref[:, h:h + 1]
        b_c = gc_ref[:, H + h:H + h + 1]
        li_r = gr_ref[h:h + 1, :]
        b_r = gr_ref[H + h:H + h + 1, :]
        m_prev = m_ref[h, 0:1, 0:1]
        C = c_ref[h]
        n = n_ref[h]

        dlog = jnp.where(causal, b_c - b_r + li_r, NEG_BIG)
        inter = b_c + m_prev
        m_comb = jnp.maximum(inter, jnp.max(dlog, axis=-1, keepdims=True))
        w_intra = jnp.exp(dlog - m_comb)
        w_inter = jnp.exp(inter - m_comb)
        s = _dot_nt(q, k) * w_intra
        num = w_inter * _dot(q, C.astype(BF16)) + _dot(s.astype(BF16), v)
        qf = q.astype(F32)
        den = w_inter * jnp.sum(qf * n, axis=-1, keepdims=True) + jnp.sum(s, axis=-1, keepdims=True)
        hh = num / jnp.maximum(jnp.abs(den), jnp.exp(-m_comb))

        b_last = b_c[L - 1:L, :]
        g_c = b_last - b_c + li_c
        m_new = jnp.maximum(b_last + m_prev, jnp.max(g_c, axis=0, keepdims=True))
        wg_c = jnp.exp(g_c - m_new)
        decay = jnp.exp(b_last + m_prev - m_new)
        kw = k.astype(F32) * wg_c
        c_ref[h] = decay * C + _dot(kw.T.astype(BF16), v)
        n_ref[h] = decay * n + jnp.sum(kw, axis=0, keepdims=True)
        m_ref[h] = jnp.broadcast_to(m_new, m_ref.shape[1:])

        mu = jnp.mean(hh, axis=-1, keepdims=True)
        d = hh - mu
        var = jnp.mean(d * d, axis=-1, keepdims=True)
        hn = d * lax.rsqrt(var + EPS) * gn_ref[:, sl]
        hn = hn + skip_ref[:, sl] * xc_ref[:, sl].astype(F32)
        o_ref[:, sl] = (hn * _silu(z_ref[:, sl].astype(F32))).astype(o_ref.dtype)


def _mlstm_scan(q, k, v, xc, proj, col_z, gc, gr, gn_g, skip, B, S, *, L):
    T, D = q.shape
    H = MLSTM_HEADS
    DH = D // H
    nc = S // L
    cz = col_z // D
    row_blk = lambda b_, i: (b_ * nc + i, 0)
    const2 = lambda b_, i: (0, 0)
    return pl.pallas_call(
        functools.partial(_mlstm_scan_kernel, L=L),
        grid=(B, nc),
        in_specs=[
            pl.BlockSpec((L, D), row_blk), pl.BlockSpec((L, D), row_blk),
            pl.BlockSpec((L, D), row_blk), pl.BlockSpec((L, D), row_blk),
            pl.BlockSpec((L, D), lambda b_, i: (b_ * nc + i, cz)),
            pl.BlockSpec((L, LANES), row_blk),
            pl.BlockSpec((2 * H, L), lambda b_, i: (0, b_ * nc + i)),
            pl.BlockSpec((1, D), const2),
            pl.BlockSpec((1, D), const2),
        ],
        out_specs=pl.BlockSpec((L, D), row_blk),
        out_shape=jax.ShapeDtypeStruct((T, D), BF16),
        scratch_shapes=[pltpu.VMEM((H, DH, DH), F32), pltpu.VMEM((H, 1, DH), F32),
                        pltpu.VMEM((H, 8, LANES), F32)],
        compiler_params=_params("parallel", "arbitrary"),
        name="mlstm_scan",
    )(q, k, v, xc, proj, gc, gr, gn_g.reshape(1, D), skip.reshape(1, D))


def _merge_mix_kernel(c_ref, a_ref, m_ref, g0_ref, g1_ref, g2_ref, bg_ref, wc_ref, wa_ref, wm_ref,
                      wmix_ref, x_ref, o_ref):
    N = x_ref.shape[1]

    def gate(t, g_ref):
        return _sigmoid(g_ref[...].astype(F32) + bg_ref[:, t * N:(t + 1) * N])

    y = gate(0, g0_ref) * _dot(c_ref[...], wc_ref[...])
    y = y + gate(1, g1_ref) * _dot(a_ref[...], wa_ref[...])
    y = y + gate(2, g2_ref) * _dot(m_ref[...], wm_ref[...])
    o_ref[...] = x_ref[...] + _dot(y.astype(BF16), wmix_ref[...])


def _layer_spec(w, layer, **kw):
    return pl.BlockSpec((None,) + w.shape[1:], lambda *idx: (layer, 0, 0), **kw)


def _merge_mix(c, a, m, proj, col_gate, b_gate, wc, wa, wm, wmix, layer, x, *, bm=256):
    T, K = c.shape
    N = wc.shape[2]
    bm = min(bm, T)
    assert col_gate % N == 0
    gcb = col_gate // N
    row = lambda i: (i, 0)
    xs = pl.BlockSpec((bm, K), row)
    once = dict(pipeline_mode=pl.Buffered(1))
    return pl.pallas_call(
        _merge_mix_kernel,
        grid=(T // bm,),
        in_specs=[xs, xs, xs,
                  pl.BlockSpec((bm, N), lambda i: (i, gcb)),
                  pl.BlockSpec((bm, N), lambda i: (i, gcb + 1)),
                  pl.BlockSpec((bm, N), lambda i: (i, gcb + 2)),
                  pl.BlockSpec((1, 3 * N), lambda i: (0, 0)),
                  _layer_spec(wc, layer, **once), _layer_spec(wa, layer, **once),
                  _layer_spec(wm, layer, **once), _layer_spec(wmix, layer, **once),
                  pl.BlockSpec((bm, N), row)],
        out_specs=pl.BlockSpec((bm, N), row),
        out_shape=jax.ShapeDtypeStruct((T, N), F32),
        compiler_params=_params("parallel"),
        name="merge_mix",
    )(c, a, m, proj, proj, proj, b_gate.reshape(1, 3 * N), wc, wa, wm, wmix, x)


def _ffn_kernel(x_ref, g_ref, gn_ref, wg_ref, wu_ref, wo_ref, o_ref, h_ref, *, emit_next):
    j = pl.program_id(1)

    @pl.when(j == 0)
    def _():
        xf = x_ref[...]
        h_ref[...] = (_rms(xf, xf.shape[-1]) * g_ref[...]).astype(BF16)
        o_ref[...] = xf

    h = h_ref[...]
    y = (_silu(_dot(h, wg_ref[...])) * _dot(h, wu_ref[...])).astype(BF16)
    o_ref[...] += _dot(y, wo_ref[...])

    if emit_next:
        @pl.when(j == pl.num_programs(1) - 1)
        def _():
            of = o_ref[...]
            h_ref[...] = (_rms(of, of.shape[-1]) * gn_ref[...]).astype(BF16)


def _ffn(x, gain, w_in, w_out, layer, next_gain=None, *, bm=1024, bh=512):
    T, D = x.shape
    F = w_out.shape[1]
    bm, bh = min(bm, T), min(bh, F)
    nh = F // bh
    emit = next_gain is not None
    gn = (next_gain if emit else gain).reshape(1, D).astype(F32)
    row = pl.BlockSpec((bm, D), lambda i, j: (i, 0))
    vec = pl.BlockSpec((1, D), lambda i, j: (0, 0))
    out_shape = [jax.ShapeDtypeStruct((T, D), F32)]
    out_specs = [row]
    scratch = [pltpu.VMEM((bm, D), BF16)]
    if emit:
        out_shape.append(jax.ShapeDtypeStruct((T, D), BF16))
        out_specs.append(row)
        scratch = []
    res = pl.pallas_call(
        functools.partial(_ffn_kernel, emit_next=emit),
        grid=(T // bm, nh),
        in_specs=[row, vec, vec,
                  pl.BlockSpec((None, D, bh), lambda i, j: (layer, 0, j)),
                  pl.BlockSpec((None, D, bh), lambda i, j: (layer, 0, j + nh)),
                  pl.BlockSpec((None, bh, D), lambda i, j: (layer, j, 0))],
        out_specs=out_specs,
        out_shape=out_shape,
        scratch_shapes=scratch,
        compiler_params=_params("parallel", "arbitrary"),
        name="ffn",
    )(x, gain.reshape(1, D).astype(F32), gn, w_in, w_in, w_out)
    return (res[0], res[1]) if emit else (res[0], None)


def _norm_kernel(x_ref, g_ref, o_ref):
    xf = x_ref[...]
    o_ref[...] = (_rms(xf, xf.shape[-1]) * g_ref[...]).astype(o_ref.dtype)


def _rmsnorm(x, gain, *, bm=1024):
    T, D = x.shape
    bm = min(bm, T)
    return pl.pallas_call(
        _norm_kernel,
        grid=(T // bm,),
        in_specs=[pl.BlockSpec((bm, D), lambda i: (i, 0)), pl.BlockSpec((1, D), lambda i: (0, 0))],
        out_specs=pl.BlockSpec((bm, D), lambda i: (i, 0)),
        out_shape=jax.ShapeDtypeStruct((T, D), BF16),
        compiler_params=_params("parallel"),
        name="rmsnorm",
    )(x, gain.reshape(1, D).astype(F32))


def _mem_kv_kernel(mem_ref, g_ref, w_ref, gk_ref, k_ref, v_ref):
    H, DH = XATTN_HEADS, XATTN_HEAD_DIM
    x = mem_ref[...]
    h = (_rms(x, x.shape[-1]) * g_ref[...]).astype(BF16)
    kv = _dot(h, w_ref[...])
    for hd in range(H):
        kh = kv[:, hd * DH:(hd + 1) * DH]
        k_ref[:, hd * DH:(hd + 1) * DH] = (_rms(kh, DH) * gk_ref[...]).astype(BF16)
    v_ref[...] = kv[:, H * DH:].astype(BF16)


def _mem_kv(mem2d, norm_mem, w_xkv, g_k, *, bm=256):
    M, D = mem2d.shape
    HD = XATTN_HEADS * XATTN_HEAD_DIM
    bm = min(bm, M)
    const = lambda i: (0, 0)
    out = jax.ShapeDtypeStruct((M, HD), BF16)
    return pl.pallas_call(
        _mem_kv_kernel,
        grid=(M // bm,),
        in_specs=[pl.BlockSpec((bm, D), lambda i: (i, 0)), pl.BlockSpec((1, D), const),
                  pl.BlockSpec((D, 2 * HD), const), pl.BlockSpec((1, XATTN_HEAD_DIM), const)],
        out_specs=[pl.BlockSpec((bm, HD), lambda i: (i, 0)), pl.BlockSpec((bm, HD), lambda i: (i, 0))],
        out_shape=[out, out],
        compiler_params=_params("parallel"),
        name="xattn_mem_kv",
    )(mem2d, norm_mem.reshape(1, D), w_xkv.astype(BF16), g_k.reshape(1, -1))


def _xattn_kernel(x_ref, g_ref, wq_ref, gq_ref, k_ref, v_ref, wo_ref, o_ref, att_ref, *, scale):
    H, DH = XATTN_HEADS, XATTN_HEAD_DIM
    x = x_ref[...]
    h = (_rms(x, x.shape[-1]) * g_ref[...]).astype(BF16)
    q = _dot(h, wq_ref[...])
    for hd in range(H):
        sl = slice(hd * DH, (hd + 1) * DH)
        qh = (_rms(q[:, sl], DH) * gq_ref[...] * scale).astype(BF16)
        s = _dot_nt(qh, k_ref[:, sl])
        e = jnp.exp(s - jnp.max(s, axis=-1, keepdims=True))
        o = _dot(e.astype(BF16), v_ref[:, sl]) / jnp.sum(e, axis=-1, keepdims=True)
        att_ref[:, sl] = o.astype(BF16)
    o_ref[...] = x + _dot(att_ref[...], wo_ref[...])


def _xattn(x, norm_x, w_xq, g_q, k, v, w_xo, B, S, *, bm=512):
    T, D = x.shape
    HD = XATTN_HEADS * XATTN_HEAD_DIM
    M = k.shape[0] // B
    bm = min(bm, S)
    nt = S // bm
    const = lambda b_, i: (0, 0)
    return pl.pallas_call(
        functools.partial(_xattn_kernel, scale=float(XATTN_HEAD_DIM ** -0.5)),
        grid=(B, nt),
        in_specs=[
            pl.BlockSpec((bm, D), lambda b_, i: (b_ * nt + i, 0)),
            pl.BlockSpec((1, D), const),
            pl.BlockSpec((D, HD), const),
            pl.BlockSpec((1, XATTN_HEAD_DIM), const),
            pl.BlockSpec((M, HD), lambda b_, i: (b_, 0)),
            pl.BlockSpec((M, HD), lambda b_, i: (b_, 0)),
            pl.BlockSpec((HD, D), const),
        ],
        out_specs=pl.BlockSpec((bm, D), lambda b_, i: (b_ * nt + i, 0)),
        out_shape=jax.ShapeDtypeStruct((T, D), F32),
        scratch_shapes=[pltpu.VMEM((bm, HD), BF16)],
        compiler_params=_params("parallel", "arbitrary"),
        name="xattn",
    )(x, norm_x.reshape(1, D), w_xq.astype(BF16), g_q.reshape(1, -1), k, v, w_xo.astype(BF16))


def _layout_w_in(w_in, C, q_lora, kv_lora, D_ml):
    n_head = 2 * C + q_lora + kv_lora + ROPE_DIM
    tail = w_in[..., n_head:].astype(BF16)
    kr = w_in[..., n_head - ROPE_DIM:n_head].astype(BF16)
    half = ROPE_DIM // 2
    zpad = jnp.zeros(kr.shape[:-1] + (LANES - ROPE_DIM,), BF16)
    head = jnp.concatenate([w_in[..., :n_head].astype(BF16), zpad, kr[..., half:], kr[..., :half], zpad],
                           axis=-1)
    o_kr = n_head - ROPE_DIM
    cols = dict(a=0, g=C, cq=2 * C, ckv=2 * C + q_lora, kr=o_kr, krs=o_kr + LANES,
                xm=0, z=D_ml, gate=2 * D_ml)
    return head, tail, cols


def kernel(x, mem, positions, norm_mix, w_in, b_gate, conv_dw, conv_dw_b, conv_ln_g, conv_ln_b, w_conv_out, mla_q_norm, mla_kv_norm, w_q_up, w_kv_up, mla_g_q, mla_g_k, w_mla_out, mlstm_conv_w, mlstm_conv_b, w_mq, w_mk, w_mv, w_if, b_if, mlstm_gn_g, mlstm_skip, w_mlstm_out, w_mix_out, norm_x, norm_mem, w_xq, w_xkv, xattn_g_q, xattn_g_k, w_xo, norm_ffn, w_ffn_in, w_ffn_out):
    B, S, D = x.shape
    T = B * S
    depth = w_in.shape[0]
    C = conv_dw.shape[2]
    D_ml = mlstm_conv_w.shape[2]
    xt = x.reshape(T, D)
    mem2d = mem.reshape(-1, D)
    cs1, cs2 = _rope_tables(positions.reshape(T, 1))
    w_head, w_tail, cols = _layout_w_in(w_in, C, w_q_up.shape[1], w_kv_up.shape[1], D_ml)
    w_conv_out, w_mla_out, w_mlstm_out, w_mix_out, w_ffn_in, w_ffn_out = (
        w.astype(BF16) for w in (w_conv_out, w_mla_out, w_mlstm_out, w_mix_out, w_ffn_in, w_ffn_out))
    h = _rmsnorm(xt, norm_mix[0])
    for l in range(depth):
        L = min(MLSTM_CHUNK, S)
        ph = _matmul(h, w_head, l, bm=2048, bn=1024, name="in_proj_head")
        pt = _matmul(h, w_tail, l, bm=2048, bn=1024, name="in_proj_tail")
        c = _conv_branch(ph, cols["a"], cols["g"], C, B, S, conv_dw[l], conv_dw_b[l],
                         conv_ln_g[l], conv_ln_b[l])
        q, k, v = _mla_prep(ph, (cols["cq"], cols["ckv"], cols["kr"], cols["krs"]), cs1, cs2,
                            mla_q_norm[l], mla_kv_norm[l], w_q_up[l], w_kv_up[l], mla_g_q[l], mla_g_k[l])
        a = _flash_attention(q, k, v, B, S)
        mq, mk, mv, xc, gc, gr = _mlstm_prep(pt, cols["xm"], B, S, mlstm_conv_w[l], mlstm_conv_b[l],
                                             w_mq[l], w_mk[l], w_mv[l], w_if[l], b_if[l], L=L)
        m = _mlstm_scan(mq, mk, mv, xc, pt, cols["z"], gc, gr, mlstm_gn_g[l], mlstm_skip[l], B, S, L=L)
        xt = _merge_mix(c, a, m, pt, cols["gate"], b_gate[l], w_conv_out, w_mla_out,
                        w_mlstm_out, w_mix_out, l, xt)
        xk, xv = _mem_kv(mem2d, norm_mem[l], w_xkv[l], xattn_g_k[l])
        xt = _xattn(xt, norm_x[l], w_xq[l], xattn_g_q[l], xk, xv, w_xo[l], B, S)
        xt, h = _ffn(xt, norm_ffn[l], w_ffn_in, w_ffn_out, l,
                     next_gain=norm_mix[l + 1] if l + 1 < depth else None)
    return xt.reshape(B, S, D)
```

```python
import functools

import jax
import jax.numpy as jnp
import numpy as np
from jax import lax
from jax.experimental import pallas as pl
from jax.experimental.pallas import tpu as pltpu

F32 = jnp.float32
BF16 = jnp.bfloat16

EPS = 1e-6
CONV_WIDTH = 31
MLA_HEADS = 8
NOPE_DIM = 128
ROPE_DIM = 64
V_DIM = 128
ROPE_BASE = 10000.0
MLSTM_HEADS = 4
MLSTM_CONV_WIDTH = 4
XATTN_HEADS = 4
XATTN_HEAD_DIM = 128
LANES = 128
SUBLANES = 8
NEG_BIG = -1e30
VMEM_LIMIT = 60 * 1024 * 1024

MLSTM_CHUNK = 512
CONV_HALO = 128
CONV_LOOKBACK = 32
MLSTM_HALO = 16


def _params(*sem):
    return pltpu.CompilerParams(dimension_semantics=sem, vmem_limit_bytes=VMEM_LIMIT)


def _sigmoid(x):
    return 0.5 * jnp.tanh(0.5 * x) + 0.5


def _silu(x):
    return x * _sigmoid(x)


def _dot(a, b):
    return jnp.dot(a, b, preferred_element_type=F32)


def _dot_nt(a, b):
    return lax.dot_general(a, b, (((1,), (1,)), ((), ())), preferred_element_type=F32)


def _rms(x, n):
    ms = jnp.sum(x * x, axis=-1, keepdims=True) * (1.0 / n)
    return x * lax.rsqrt(ms + EPS)


def _mm_kernel(x_ref, w_ref, o_ref):
    o_ref[...] = _dot(x_ref[...], w_ref[...]).astype(o_ref.dtype)


def _matmul(x, w, layer, *, bm, bn, name):
    M, K = x.shape
    N = w.shape[2]
    bm, bn = min(bm, M), min(bn, N)
    assert M % bm == 0 and N % bn == 0 and w.shape[1] == K
    return pl.pallas_call(
        _mm_kernel,
        grid=(M // bm, N // bn),
        in_specs=[pl.BlockSpec((bm, K), lambda i, j: (i, 0)),
                  pl.BlockSpec((None, K, bn), lambda i, j: (layer, 0, j))],
        out_specs=pl.BlockSpec((bm, bn), lambda i, j: (i, j)),
        out_shape=jax.ShapeDtypeStruct((M, N), BF16),
        compiler_params=_params("parallel", "arbitrary"),
        name=name,
    )(x, w)


def _conv_kernel(a_ref, g_ref, p_ref, w_ref, b_ref, lg_ref, lb_ref, o_ref, u_ref, *, bt, rt):
    i = pl.program_id(1)

    @pl.when(i == 0)
    def _():
        u_ref[0:CONV_HALO, :] = jnp.zeros((CONV_HALO, u_ref.shape[1]), BF16)

    @pl.when(i > 0)
    def _():
        u_ref[0:CONV_HALO, :] = u_ref[bt:bt + CONV_HALO, :]

    a = a_ref[...].astype(F32)
    g = g_ref[...].astype(F32)
    u_ref[CONV_HALO:CONV_HALO + bt, :] = (a * _sigmoid(g)).astype(BF16)
    bias = b_ref[...]
    sb = p_ref.shape[1] - CONV_LOOKBACK
    for k0 in range(0, bt, sb):
        u_win = u_ref[k0:k0 + sb + CONV_HALO, :]
        sh = [u_win[CONV_HALO - CONV_LOOKBACK:, :].astype(F32)]
        sh += [_dot(p_ref[r - 1], u_win) for r in range(1, SUBLANES)]
        for r0 in range(0, sb, rt):
            acc = [jnp.broadcast_to(bias, (SUBLANES, bias.shape[1]))] * (rt // SUBLANES)
            for j in range(CONV_WIDTH):
                shift = CONV_WIDTH - 1 - j
                r, start = shift % SUBLANES, r0 + CONV_LOOKBACK - (shift // SUBLANES) * SUBLANES
                wj = w_ref[j]
                acc = [acc[q] + wj * sh[r][start + q * SUBLANES:start + (q + 1) * SUBLANES, :]
                       for q in range(rt // SUBLANES)]
            acc = jnp.concatenate(acc, axis=0)
            mu = jnp.mean(acc, axis=-1, keepdims=True)
            d = acc - mu
            var = jnp.mean(d * d, axis=-1, keepdims=True)
            y = d * lax.rsqrt(var + EPS) * lg_ref[...] + lb_ref[...]
            o_ref[k0 + r0:k0 + r0 + rt, :] = _silu(y).astype(o_ref.dtype)


def _shift_matrices(sb):
    m = np.arange(sb + CONV_LOOKBACK)[None, :, None]
    k = np.arange(sb + CONV_HALO)[None, None, :]
    r = np.arange(1, SUBLANES)[:, None, None]
    return jnp.asarray(k == CONV_HALO - CONV_LOOKBACK + m - r, dtype=BF16)


def _conv_branch(proj, col_a, col_g, C, B, S, w, b, ln_g, ln_b, *, bt=256, sb=128, rt=32):
    bt = min(bt, S)
    assert bt >= CONV_HALO and bt % sb == 0 and sb % rt == 0
    nt = S // bt
    ca, cg = col_a // C, col_g // C
    pmat = _shift_matrices(sb)
    w = jnp.broadcast_to(w[:, None, :], (CONV_WIDTH, SUBLANES, C))
    const = lambda b_, i: (0, 0)
    return pl.pallas_call(
        functools.partial(_conv_kernel, bt=bt, rt=rt),
        grid=(B, nt),
        in_specs=[
            pl.BlockSpec((bt, C), lambda b_, i: (b_ * nt + i, ca)),
            pl.BlockSpec((bt, C), lambda b_, i: (b_ * nt + i, cg)),
            pl.BlockSpec(pmat.shape, lambda b_, i: (0, 0, 0)),
            pl.BlockSpec((CONV_WIDTH, SUBLANES, C), lambda b_, i: (0, 0, 0)),
            pl.BlockSpec((1, C), const),
            pl.BlockSpec((1, C), const),
            pl.BlockSpec((1, C), const),
        ],
        out_specs=pl.BlockSpec((bt, C), lambda b_, i: (b_ * nt + i, 0)),
        out_shape=jax.ShapeDtypeStruct((B * S, C), BF16),
        scratch_shapes=[pltpu.VMEM((CONV_HALO + bt, C), BF16)],
        compiler_params=_params("parallel", "arbitrary"),
        name="conformer_conv",
    )(proj, proj, pmat, w, b.reshape(1, C), ln_g.reshape(1, C), ln_b.reshape(1, C))


def _rope_kernel(pos_ref, freq_ref, sgn_ref, cs1_ref, cs2_ref):
    ang = pos_ref[...].astype(F32) * freq_ref[...]
    cs1_ref[...] = jnp.cos(ang)
    cs2_ref[...] = jnp.sin(ang) * sgn_ref[...]


def _rope_tables(pos, *, bm=1024):
    T = pos.shape[0]
    bm = min(bm, T)
    half = ROPE_DIM // 2
    inv_freq = ROPE_BASE ** (-jnp.arange(half, dtype=F32) / half)
    freq = _pad_lanes(jnp.concatenate([inv_freq, inv_freq]))[None, :]
    sgn = _pad_lanes(jnp.concatenate([-jnp.ones((half,), F32), jnp.ones((half,), F32)]))[None, :]
    const = lambda i: (0, 0)
    tab = jax.ShapeDtypeStruct((T, LANES), F32)
    return pl.pallas_call(
        _rope_kernel,
        grid=(T // bm,),
        in_specs=[pl.BlockSpec((bm, 1), lambda i: (i, 0)), pl.BlockSpec((1, LANES), const),
                  pl.BlockSpec((1, LANES), const)],
        out_specs=[pl.BlockSpec((bm, LANES), lambda i: (i, 0)), pl.BlockSpec((bm, LANES), lambda i: (i, 0))],
        out_shape=[tab, tab],
        compiler_params=_params("parallel"),
        name="rope_tables",
    )(pos, freq, sgn)


def _mla_prep_kernel(cq_ref, ckv_ref, kr_ref, krs_ref, cs1_ref, cs2_ref,
                     qn_ref, kvn_ref, wq_ref, wkv_ref, gq_ref, gk_ref,
                     q_ref, k_ref, v_ref, *, scale):
    H = MLA_HEADS
    cs1 = cs1_ref[...]
    cs2 = cs2_ref[...]

    cq = cq_ref[...].astype(F32)
    hq = (_rms(cq, cq.shape[-1]) * qn_ref[...]).astype(BF16)
    qall = _dot(hq, wq_ref[...])
    ckv = ckv_ref[...].astype(F32)
    hkv = (_rms(ckv, ckv.shape[-1]) * kvn_ref[...]).astype(BF16)
    kvall = _dot(hkv, wkv_ref[...])

    g_nope_q, g_rot_q, g_rots_q = gq_ref[0:1, :], gq_ref[1:2, :], gq_ref[2:3, :]
    g_nope_k, g_rot_k, g_rots_k = gk_ref[0:1, :], gk_ref[1:2, :], gk_ref[2:3, :]

    kr = kr_ref[...].astype(F32)
    krs = krs_ref[...].astype(F32)
    rs = lax.rsqrt(jnp.sum(kr * kr, axis=-1, keepdims=True) * (1.0 / ROPE_DIM) + EPS)
    k_rot = ((kr * rs * g_rot_k) * cs1 + (krs * rs * g_rots_k) * cs2).astype(BF16)

    for h in range(H):
        qn = qall[:, h * LANES:(h + 1) * LANES]
        q_ref[h, :, 0:LANES] = (_rms(qn, NOPE_DIM) * g_nope_q * scale).astype(BF16)
        qr = qall[:, (H + h) * LANES:(H + h + 1) * LANES]
        qs = qall[:, (2 * H + h) * LANES:(2 * H + h + 1) * LANES]
        rq = lax.rsqrt(jnp.sum(qr * qr, axis=-1, keepdims=True) * (1.0 / ROPE_DIM) + EPS)
        q_rot = (qr * rq * g_rot_q) * cs1 + (qs * rq * g_rots_q) * cs2
        q_ref[h, :, LANES:2 * LANES] = (q_rot * scale).astype(BF16)
        kn = kvall[:, h * LANES:(h + 1) * LANES]
        k_ref[h, :, 0:LANES] = (_rms(kn, NOPE_DIM) * g_nope_k).astype(BF16)
        k_ref[h, :, LANES:2 * LANES] = k_rot
        v_ref[h, :, 0:V_DIM] = kvall[:, (H + h) * LANES:(H + h + 1) * LANES].astype(BF16)
        v_ref[h, :, V_DIM:V_DIM + LANES] = jnp.ones((kvall.shape[0], LANES), BF16)


def _pad_lanes(v, n=LANES):
    return jnp.pad(v, [(0, 0)] * (v.ndim - 1) + [(0, n - v.shape[-1])])


def _swap_halves(v):
    half = v.shape[-1] // 2
    return jnp.concatenate([v[..., half:], v[..., :half]], axis=-1)


def _mla_prep(proj, cols, cs1, cs2, mla_q_norm, mla_kv_norm, w_q_up, w_kv_up, g_q, g_k, *, bm=512):
    T = proj.shape[0]
    bm = min(bm, T)
    H = MLA_HEADS
    q_lora, kv_lora = w_q_up.shape[0], w_kv_up.shape[0]
    c_cq, c_ckv, c_kr, c_krs = cols
    wq = w_q_up.reshape(q_lora, H, NOPE_DIM + ROPE_DIM)
    wq_rot = wq[:, :, NOPE_DIM:]
    wq_p = jnp.concatenate([
        wq[:, :, :NOPE_DIM].reshape(q_lora, H * NOPE_DIM),
        _pad_lanes(wq_rot).reshape(q_lora, H * LANES),
        _pad_lanes(_swap_halves(wq_rot)).reshape(q_lora, H * LANES)], axis=1).astype(BF16)
    wkv = w_kv_up.reshape(kv_lora, H, NOPE_DIM + V_DIM)
    wkv_p = jnp.concatenate([wkv[:, :, :NOPE_DIM].reshape(kv_lora, H * NOPE_DIM),
                             wkv[:, :, NOPE_DIM:].reshape(kv_lora, H * V_DIM)], axis=1).astype(BF16)

    def gains(g):
        return jnp.stack([g[:NOPE_DIM], _pad_lanes(g[NOPE_DIM:]), _pad_lanes(_swap_halves(g[NOPE_DIM:]))])

    scale = float((NOPE_DIM + ROPE_DIM) ** -0.5 * np.log2(np.e))
    const = lambda i: (0, 0)
    return pl.pallas_call(
        functools.partial(_mla_prep_kernel, scale=scale),
        grid=(T // bm,),
        in_specs=[
            pl.BlockSpec((bm, q_lora), lambda i: (i, c_cq // q_lora)),
            pl.BlockSpec((bm, kv_lora), lambda i: (i, c_ckv // kv_lora)),
            pl.BlockSpec((bm, LANES), lambda i: (i, c_kr // LANES)),
            pl.BlockSpec((bm, LANES), lambda i: (i, c_krs // LANES)),
            pl.BlockSpec((bm, LANES), lambda i: (i, 0)),
            pl.BlockSpec((bm, LANES), lambda i: (i, 0)),
            pl.BlockSpec((1, q_lora), const),
            pl.BlockSpec((1, kv_lora), const),
            pl.BlockSpec(wq_p.shape, const),
            pl.BlockSpec(wkv_p.shape, const),
            pl.BlockSpec((3, LANES), const),
            pl.BlockSpec((3, LANES), const),
        ],
        out_specs=[
            pl.BlockSpec((H, bm, 2 * LANES), lambda i: (0, i, 0)),
            pl.BlockSpec((H, bm, 2 * LANES), lambda i: (0, i, 0)),
            pl.BlockSpec((H, bm, V_DIM + LANES), lambda i: (0, i, 0)),
        ],
        out_shape=[
            jax.ShapeDtypeStruct((H, T, 2 * LANES), BF16),
            jax.ShapeDtypeStruct((H, T, 2 * LANES), BF16),
            jax.ShapeDtypeStruct((H, T, V_DIM + LANES), BF16),
        ],
        compiler_params=_params("parallel"),
        name="mla_prep",
    )(proj, proj, proj, proj, cs1, cs2, mla_q_norm.reshape(1, -1), mla_kv_norm.reshape(1, -1),
      wq_p, wkv_p, gains(g_q), gains(g_k))


def _flash_kernel(q_ref, k_ref, v_ref, o_ref, m_ref, acc_ref, *, bh, nh):
    i = pl.program_id(2)
    m_ref[...] = jnp.full(m_ref.shape, NEG_BIG, F32)
    acc_ref[...] = jnp.zeros(acc_ref.shape, F32)
    nc = bh // LANES

    def chain(a, j, masked):
        rows = slice(a * bh, (a + 1) * bh)
        start = pl.multiple_of(j * bh, bh)
        k = k_ref[0, pl.ds(start, bh), :]
        v = v_ref[0, pl.ds(start, bh), :]
        s = _dot_nt(q_ref[0, rows, :], k)
        if masked:
            row = lax.broadcasted_iota(jnp.int32, s.shape, 0)
            col = lax.broadcasted_iota(jnp.int32, s.shape, 1)
            s = jnp.where(col <= row, s, NEG_BIG)
        chunks = [s[:, c * LANES:(c + 1) * LANES] for c in range(nc)]
        mx = functools.reduce(jnp.maximum, chunks)
        m_prev = m_ref[rows, :]
        m_new = jnp.maximum(m_prev, jnp.max(mx, axis=-1, keepdims=True))
        alpha = jnp.exp2(m_prev - m_new)
        p = jnp.concatenate([jnp.exp2(c - m_new) for c in chunks], axis=1).astype(BF16)
        alpha2 = jnp.concatenate([alpha] * (acc_ref.shape[1] // LANES), axis=1)
        acc_ref[rows, :] = alpha2 * acc_ref[rows, :] + _dot(p, v)
        m_ref[rows, :] = m_new

    def body(j, carry):
        for a in range(nh):
            chain(a, j, False)
        return carry

    lax.fori_loop(0, i * nh, body, 0)
    for a in range(nh):
        for c in range(a + 1):
            chain(a, i * nh + c, c == a)
    o_ref[...] = (acc_ref[:, 0:V_DIM] / acc_ref[:, V_DIM:2 * V_DIM]).astype(o_ref.dtype)


def _flash_attention(q, k, v, B, S, *, bh=512, nh=8):
    H = q.shape[0]
    bh = min(bh, S // nh)
    bq = bh * nh
    nq = S // bq
    return pl.pallas_call(
        functools.partial(_flash_kernel, bh=bh, nh=nh),
        grid=(B, H, nq),
        in_specs=[
            pl.BlockSpec((1, bq, q.shape[2]), lambda b, h, i: (h, b * nq + i, 0)),
            pl.BlockSpec((1, S, k.shape[2]), lambda b, h, i: (h, b, 0)),
            pl.BlockSpec((1, S, v.shape[2]), lambda b, h, i: (h, b, 0)),
        ],
        out_specs=pl.BlockSpec((bq, V_DIM), lambda b, h, i: (b * nq + i, h)),
        out_shape=jax.ShapeDtypeStruct((B * S, H * V_DIM), BF16),
        scratch_shapes=[pltpu.VMEM((bq, LANES), F32), pltpu.VMEM((bq, v.shape[2]), F32)],
        compiler_params=_params("parallel", "parallel", "arbitrary"),
        name="mla_flash",
    )(q, k, v)


def _log_sigmoid(x):
    return -(jnp.maximum(-x, 0.0) + jnp.log(1.0 + jnp.exp(-jnp.abs(x))))


def _split_dot(a, b_exact):
    hi = a.astype(BF16)
    lo = (a - hi.astype(F32)).astype(BF16)
    return _dot(hi, b_exact) + _dot(lo, b_exact)


def _mlstm_prep_kernel(xm_ref, halo_ref, cw_ref, cb_ref, wq_ref, wk_ref, wv_ref, wif_ref, wift_ref,
                       bif_ref, bift_ref, q_ref, k_ref, v_ref, xc_ref, gc_ref, gr_ref,
                       buf_ref, qkv_ref, *, bm, L):
    i = pl.program_id(1)
    H = MLSTM_HEADS
    D = xm_ref.shape[1]
    DH = D // H
    halo = halo_ref[...].astype(F32)
    buf_ref[0:MLSTM_HALO, :] = jnp.where(i > 0, halo, 0.0)
    xm_bf = xm_ref[...]
    buf_ref[MLSTM_HALO:MLSTM_HALO + bm, :] = xm_bf.astype(F32)
    off = MLSTM_HALO - (MLSTM_CONV_WIDTH - 1)
    acc = jnp.broadcast_to(cb_ref[...], (bm, D))
    for j in range(MLSTM_CONV_WIDTH):
        acc = acc + cw_ref[j:j + 1, :] * buf_ref[off + j:off + j + bm, :]
    xc = _silu(acc)
    xc_bf = xc.astype(BF16)
    xc_ref[...] = xc_bf
    for h in range(H):
        sl = slice(h * DH, (h + 1) * DH)
        qh = _dot(xc_bf[:, sl], wq_ref[h]).astype(BF16)
        k_raw = _dot(xc_bf[:, sl], wk_ref[h])
        vh = _dot(xm_bf[:, sl], wv_ref[h]).astype(BF16)
        q_ref[:, sl] = qh
        k_ref[:, sl] = (k_raw * (DH ** -0.5)).astype(BF16)
        v_ref[:, sl] = vh
        qkv_ref[:, h * DH:(h + 1) * DH] = qh
        qkv_ref[:, D + h * DH:D + (h + 1) * DH] = k_raw.astype(BF16)
        qkv_ref[:, 2 * D + h * DH:2 * D + (h + 1) * DH] = vh
    qkv = qkv_ref[...]
    pre_c = _dot(qkv, wif_ref[...]) + bif_ref[...]
    lane = lax.broadcasted_iota(jnp.int32, pre_c.shape, 1)
    is_f_c = (lane >= H) & (lane < 2 * H)
    lf_c = jnp.where(is_f_c, _log_sigmoid(pre_c), 0.0)
    r = lax.broadcasted_iota(jnp.int32, (bm, bm), 0)
    c = lax.broadcasted_iota(jnp.int32, (bm, bm), 1)
    same = (r // L) == (c // L)
    tri = jnp.where(same & (c <= r), 1.0, 0.0).astype(BF16)
    cum_c = _split_dot_left(tri, lf_c)
    gc_ref[...] = jnp.where(is_f_c, cum_c, pre_c)
    pre_r = _dot_nt(wift_ref[...], qkv) + bift_ref[...]
    row = lax.broadcasted_iota(jnp.int32, pre_r.shape, 0)
    is_f_r = row >= H
    lf_r = jnp.where(is_f_r, _log_sigmoid(pre_r), 0.0)
    trit = jnp.where(same & (r <= c), 1.0, 0.0).astype(BF16)
    cum_r = _split_dot(lf_r, trit)
    gr_ref[...] = jnp.where(is_f_r, cum_r, pre_r)


def _split_dot_left(a_exact, b):
    hi = b.astype(BF16)
    lo = (b - hi.astype(F32)).astype(BF16)
    return _dot(a_exact, hi) + _dot(a_exact, lo)


def _mlstm_prep(proj, col_xm, B, S, conv_w, conv_b, w_mq, w_mk, w_mv, w_if, b_if, *, bm=512, L):
    T = B * S
    H = MLSTM_HEADS
    D = conv_w.shape[1]
    bm = min(bm, S)
    assert bm % L == 0
    nt = S // bm
    cxm = col_xm // D
    hb = bm // MLSTM_HALO
    wif = _pad_lanes(w_if).astype(BF16)
    wift = w_if.T.astype(BF16)
    bif = _pad_lanes(b_if)[None, :]
    bift = b_if[:, None]
    const2 = lambda b_, i: (0, 0)
    const3 = lambda b_, i: (0, 0, 0)
    row_blk = lambda b_, i: (b_ * nt + i, 0)
    act = jax.ShapeDtypeStruct((T, D), BF16)
    return pl.pallas_call(
        functools.partial(_mlstm_prep_kernel, bm=bm, L=L),
        grid=(B, nt),
        in_specs=[
            pl.BlockSpec((bm, D), lambda b_, i: (b_ * nt + i, cxm)),
            pl.BlockSpec((MLSTM_HALO, D), lambda b_, i: (jnp.maximum((b_ * nt + i) * hb - 1, 0), cxm)),
            pl.BlockSpec((MLSTM_CONV_WIDTH, D), const2),
            pl.BlockSpec((1, D), const2),
            pl.BlockSpec(w_mq.shape, const3),
            pl.BlockSpec(w_mk.shape, const3),
            pl.BlockSpec(w_mv.shape, const3),
            pl.BlockSpec(wif.shape, const2),
            pl.BlockSpec(wift.shape, const2),
            pl.BlockSpec((1, LANES), const2),
            pl.BlockSpec((2 * H, 1), const2),
        ],
        out_specs=[
            pl.BlockSpec((bm, D), row_blk), pl.BlockSpec((bm, D), row_blk),
            pl.BlockSpec((bm, D), row_blk), pl.BlockSpec((bm, D), row_blk),
            pl.BlockSpec((bm, LANES), row_blk),
            pl.BlockSpec((2 * H, bm), lambda b_, i: (0, b_ * nt + i)),
        ],
        out_shape=[act, act, act, act,
                   jax.ShapeDtypeStruct((T, LANES), F32),
                   jax.ShapeDtypeStruct((2 * H, T), F32)],
        scratch_shapes=[pltpu.VMEM((MLSTM_HALO + bm, D), F32), pltpu.VMEM((bm, 3 * D), BF16)],
        compiler_params=_params("parallel", "arbitrary"),
        name="mlstm_prep",
    )(proj, proj, conv_w, conv_b.reshape(1, D), w_mq.astype(BF16), w_mk.astype(BF16),
      w_mv.astype(BF16), wif, wift, bif, bift)


def _mlstm_scan_kernel(q_ref, k_ref, v_ref, xc_ref, z_ref, gc_ref, gr_ref, gn_ref, skip_ref,
                       o_ref, c_ref, n_ref, m_ref, *, L):
    H = MLSTM_HEADS
    DH = q_ref.shape[1] // H

    @pl.when(pl.program_id(1) == 0)
    def _():
        c_ref[...] = jnp.zeros(c_ref.shape, F32)
        n_ref[...] = jnp.zeros(n_ref.shape, F32)
        m_ref[...] = jnp.zeros(m_ref.shape, F32)

    row = lax.broadcasted_iota(jnp.int32, (L, L), 0)
    col = lax.broadcasted_iota(jnp.int32, (L, L), 1)
    causal = col <= row
    for h in range(H):
        sl = slice(h * DH, (h + 1) * DH)
        q = q_ref[:, sl]
        k = k_ref[:, sl]
        v = v_ref[:, sl]
        li_c = gc_ref[:, h:h + 1]
        b_c = gc_ref[:, H + h:H + h + 1]
        li_r = gr_ref[h:h + 1, :]
        b_r = gr_ref[H + h:H + h + 1, :]
        m_prev = m_ref[h, 0:1, 0:1]
        C = c_ref[h]
        n = n_ref[h]

        dlog = jnp.where(causal, b_c - b_r + li_r, NEG_BIG)
        inter = b_c + m_prev
        m_comb = jnp.maximum(inter, jnp.max(dlog, axis=-1, keepdims=True))
        w_intra = jnp.exp(dlog - m_comb)
        w_inter = jnp.exp(inter - m_comb)
        s = _dot_nt(q, k) * w_intra
        num = w_inter * _dot(q, C.astype(BF16)) + _dot(s.astype(BF16), v)
        qf = q.astype(F32)
        den = w_inter * jnp.sum(qf * n, axis=-1, keepdims=True) + jnp.sum(s, axis=-1, keepdims=True)
        hh = num / jnp.maximum(jnp.abs(den), jnp.exp(-m_comb))

        b_last = b_c[L - 1:L, :]
        g_c = b_last - b_c + li_c
        m_new = jnp.maximum(b_last + m_prev, jnp.max(g_c, axis=0, keepdims=True))
        wg_c = jnp.exp(g_c - m_new)
        decay = jnp.exp(b_last + m_prev - m_new)
        kw = k.astype(F32) * wg_c
        c_ref[h] = decay * C + _dot(kw.T.astype(BF16), v)
        n_ref[h] = decay * n + jnp.sum(kw, axis=0, keepdims=True)
        m_ref[h] = jnp.broadcast_to(m_new, m_ref.shape[1:])

        mu = jnp.mean(hh, axis=-1, keepdims=True)
        d = hh - mu
        var = jnp.mean(d * d, axis=-1, keepdims=True)
        hn = d * lax.rsqrt(var + EPS) * gn_ref[:, sl]
        hn = hn + skip_ref[:, sl] * xc_ref[:, sl].astype(F32)
        o_ref[:, sl] = (hn * _silu(z_ref[:, sl].astype(F32))).astype(o_ref.dtype)


def _mlstm_scan(q, k, v, xc, proj, col_z, gc, gr, gn_g, skip, B, S, *, L):
    T, D = q.shape
    H = MLSTM_HEADS
    DH = D // H
    nc = S // L
    cz = col_z // D
    row_blk = lambda b_, i: (b_ * nc + i, 0)
    const2 = lambda b_, i: (0, 0)
    return pl.pallas_call(
        functools.partial(_mlstm_scan_kernel, L=L),
        grid=(B, nc),
        in_specs=[
            pl.BlockSpec((L, D), row_blk), pl.BlockSpec((L, D), row_blk),
            pl.BlockSpec((L, D), row_blk), pl.BlockSpec((L, D), row_blk),
            pl.BlockSpec((L, D), lambda b_, i: (b_ * nc + i, cz)),
            pl.BlockSpec((L, LANES), row_blk),
            pl.BlockSpec((2 * H, L), lambda b_, i: (0, b_ * nc + i)),
            pl.BlockSpec((1, D), const2),
            pl.BlockSpec((1, D), const2),
        ],
        out_specs=pl.BlockSpec((L, D), row_blk),
        out_shape=jax.ShapeDtypeStruct((T, D), BF16),
        scratch_shapes=[pltpu.VMEM((H, DH, DH), F32), pltpu.VMEM((H, 1, DH), F32),
                        pltpu.VMEM((H, 8, LANES), F32)],
        compiler_params=_params("parallel", "arbitrary"),
        name="mlstm_scan",
    )(q, k, v, xc, proj, gc, gr, gn_g.reshape(1, D), skip.reshape(1, D))


def _merge_mix_kernel(c_ref, a_ref, m_ref, g0_ref, g1_ref, g2_ref, bg_ref, wc_ref, wa_ref, wm_ref,
                      wmix_ref, x_ref, o_ref):
    N = x_ref.shape[1]

    def gate(t, g_ref):
        return _sigmoid(g_ref[...].astype(F32) + bg_ref[:, t * N:(t + 1) * N])

    y = gate(0, g0_ref) * _dot(c_ref[...], wc_ref[...])
    y = y + gate(1, g1_ref) * _dot(a_ref[...], wa_ref[...])
    y = y + gate(2, g2_ref) * _dot(m_ref[...], wm_ref[...])
    o_ref[...] = x_ref[...] + _dot(y.astype(BF16), wmix_ref[...])


def _layer_spec(w, layer, **kw):
    return pl.BlockSpec((None,) + w.shape[1:], lambda *idx: (layer, 0, 0), **kw)


def _merge_mix(c, a, m, proj, col_gate, b_gate, wc, wa, wm, wmix, layer, x, *, bm=256):
    T, K = c.shape
    N = wc.shape[2]
    bm = min(bm, T)
    assert col_gate % N == 0
    gcb = col_gate // N
    row = lambda i: (i, 0)
    xs = pl.BlockSpec((bm, K), row)
    once = dict(pipeline_mode=pl.Buffered(1))
    return pl.pallas_call(
        _merge_mix_kernel,
        grid=(T // bm,),
        in_specs=[xs, xs, xs,
                  pl.BlockSpec((bm, N), lambda i: (i, gcb)),
                  pl.BlockSpec((bm, N), lambda i: (i, gcb + 1)),
                  pl.BlockSpec((bm, N), lambda i: (i, gcb + 2)),
                  pl.BlockSpec((1, 3 * N), lambda i: (0, 0)),
                  _layer_spec(wc, layer, **once), _layer_spec(wa, layer, **once),
                  _layer_spec(wm, layer, **once), _layer_spec(wmix, layer, **once),
                  pl.BlockSpec((bm, N), row)],
        out_specs=pl.BlockSpec((bm, N), row),
        out_shape=jax.ShapeDtypeStruct((T, N), F32),
        compiler_params=_params("parallel"),
        name="merge_mix",
    )(c, a, m, proj, proj, proj, b_gate.reshape(1, 3 * N), wc, wa, wm, wmix, x)


def _ffn_kernel(x_ref, g_ref, gn_ref, wg_ref, wu_ref, wo_ref, o_ref, h_ref, *, emit_next):
    j = pl.program_id(1)

    @pl.when(j == 0)
    def _():
        xf = x_ref[...]
        h_ref[...] = (_rms(xf, xf.shape[-1]) * g_ref[...]).astype(BF16)
        o_ref[...] = xf

    h = h_ref[...]
    y = (_silu(_dot(h, wg_ref[...])) * _dot(h, wu_ref[...])).astype(BF16)
    o_ref[...] += _dot(y, wo_ref[...])

    if emit_next:
        @pl.when(j == pl.num_programs(1) - 1)
        def _():
            of = o_ref[...]
            h_ref[...] = (_rms(of, of.shape[-1]) * gn_ref[...]).astype(BF16)


def _ffn(x, gain, w_in, w_out, layer, next_gain=None, *, bm=1024, bh=512):
    T, D = x.shape
    F = w_out.shape[1]
    bm, bh = min(bm, T), min(bh, F)
    nh = F // bh
    emit = next_gain is not None
    gn = (next_gain if emit else gain).reshape(1, D).astype(F32)
    row = pl.BlockSpec((bm, D), lambda i, j: (i, 0))
    vec = pl.BlockSpec((1, D), lambda i, j: (0, 0))
    out_shape = [jax.ShapeDtypeStruct((T, D), F32)]
    out_specs = [row]
    scratch = [pltpu.VMEM((bm, D), BF16)]
    if emit:
        out_shape.append(jax.ShapeDtypeStruct((T, D), BF16))
        out_specs.append(row)
        scratch = []
    res = pl.pallas_call(
        functools.partial(_ffn_kernel, emit_next=emit),
        grid=(T // bm, nh),
        in_specs=[row, vec, vec,
                  pl.BlockSpec((None, D, bh), lambda i, j: (layer, 0, j)),
                  pl.BlockSpec((None, D, bh), lambda i, j: (layer, 0, j + nh)),
                  pl.BlockSpec((None, bh, D), lambda i, j: (layer, j, 0))],
        out_specs=out_specs,
        out_shape=out_shape,
        scratch_shapes=scratch,
        compiler_params=_params("parallel", "arbitrary"),
        name="ffn",
    )(x, gain.reshape(1, D).astype(F32), gn, w_in, w_in, w_out)
    return (res[0], res[1]) if emit else (res[0], None)


def _norm_kernel(x_ref, g_ref, o_ref):
    xf = x_ref[...]
    o_ref[...] = (_rms(xf, xf.shape[-1]) * g_ref[...]).astype(o_ref.dtype)


def _rmsnorm(x, gain, *, bm=1024):
    T, D = x.shape
    bm = min(bm, T)
    return pl.pallas_call(
        _norm_kernel,
        grid=(T // bm,),
        in_specs=[pl.BlockSpec((bm, D), lambda i: (i, 0)), pl.BlockSpec((1, D), lambda i: (0, 0))],
        out_specs=pl.BlockSpec((bm, D), lambda i: (i, 0)),
        out_shape=jax.ShapeDtypeStruct((T, D), BF16),
        compiler_params=_params("parallel"),
        name="rmsnorm",
    )(x, gain.reshape(1, D).astype(F32))


def _mem_kv_kernel(mem_ref, g_ref, w_ref, gk_ref, k_ref, v_ref):
    H, DH = XATTN_HEADS, XATTN_HEAD_DIM
    x = mem_ref[...]
    h = (_rms(x, x.shape[-1]) * g_ref[...]).astype(BF16)
    kv = _dot(h, w_ref[...])
    for hd in range(H):
        kh = kv[:, hd * DH:(hd + 1) * DH]
        k_ref[:, hd * DH:(hd + 1) * DH] = (_rms(kh, DH) * gk_ref[...]).astype(BF16)
    v_ref[...] = kv[:, H * DH:].astype(BF16)


def _mem_kv(mem2d, norm_mem, w_xkv, g_k, *, bm=256):
    M, D = mem2d.shape
    HD = XATTN_HEADS * XATTN_HEAD_DIM
    bm = min(bm, M)
    const = lambda i: (0, 0)
    out = jax.ShapeDtypeStruct((M, HD), BF16)
    return pl.pallas_call(
        _mem_kv_kernel,
        grid=(M // bm,),
        in_specs=[pl.BlockSpec((bm, D), lambda i: (i, 0)), pl.BlockSpec((1, D), const),
                  pl.BlockSpec((D, 2 * HD), const), pl.BlockSpec((1, XATTN_HEAD_DIM), const)],
        out_specs=[pl.BlockSpec((bm, HD), lambda i: (i, 0)), pl.BlockSpec((bm, HD), lambda i: (i, 0))],
        out_shape=[out, out],
        compiler_params=_params("parallel"),
        name="xattn_mem_kv",
    )(mem2d, norm_mem.reshape(1, D), w_xkv.astype(BF16), g_k.reshape(1, -1))


def _xattn_kernel(x_ref, g_ref, wq_ref, gq_ref, k_ref, v_ref, wo_ref, o_ref, att_ref, *, scale):
    H, DH = XATTN_HEADS, XATTN_HEAD_DIM
    x = x_ref[...]
    h = (_rms(x, x.shape[-1]) * g_ref[...]).astype(BF16)
    q = _dot(h, wq_ref[...])
    for hd in range(H):
        sl = slice(hd * DH, (hd + 1) * DH)
        qh = (_rms(q[:, sl], DH) * gq_ref[...] * scale).astype(BF16)
        s = _dot_nt(qh, k_ref[:, sl])
        e = jnp.exp(s - jnp.max(s, axis=-1, keepdims=True))
        o = _dot(e.astype(BF16), v_ref[:, sl]) / jnp.sum(e, axis=-1, keepdims=True)
        att_ref[:, sl] = o.astype(BF16)
    o_ref[...] = x + _dot(att_ref[...], wo_ref[...])


def _xattn(x, norm_x, w_xq, g_q, k, v, w_xo, B, S, *, bm=512):
    T, D = x.shape
    HD = XATTN_HEADS * XATTN_HEAD_DIM
    M = k.shape[0] // B
    bm = min(bm, S)
    nt = S // bm
    const = lambda b_, i: (0, 0)
    return pl.pallas_call(
        functools.partial(_xattn_kernel, scale=float(XATTN_HEAD_DIM ** -0.5)),
        grid=(B, nt),
        in_specs=[
            pl.BlockSpec((bm, D), lambda b_, i: (b_ * nt + i, 0)),
            pl.BlockSpec((1, D), const),
            pl.BlockSpec((D, HD), const),
            pl.BlockSpec((1, XATTN_HEAD_DIM), const),
            pl.BlockSpec((M, HD), lambda b_, i: (b_, 0)),
            pl.BlockSpec((M, HD), lambda b_, i: (b_, 0)),
            pl.BlockSpec((HD, D), const),
        ],
        out_specs=pl.BlockSpec((bm, D), lambda b_, i: (b_ * nt + i, 0)),
        out_shape=jax.ShapeDtypeStruct((T, D), F32),
        scratch_shapes=[pltpu.VMEM((bm, HD), BF16)],
        compiler_params=_params("parallel", "arbitrary"),
        name="xattn",
    )(x, norm_x.reshape(1, D), w_xq.astype(BF16), g_q.reshape(1, -1), k, v, w_xo.astype(BF16))


def _layout_w_in(w_in, C, q_lora, kv_lora, D_ml):
    depth, D, n_cols = w_in.shape
    n_head = 2 * C + q_lora + kv_lora + ROPE_DIM
    o_kr = n_head - ROPE_DIM
    half = ROPE_DIM // 2
    n_head_p = n_head + 3 * (LANES - ROPE_DIM)
    bm = min(128, D)

    def relayout(w_ref, head_ref, tail_ref):
        w = w_ref[...]
        zpad = jnp.zeros((w.shape[0], LANES - ROPE_DIM), F32)
        head = jnp.concatenate([w[:, :n_head], zpad, w[:, o_kr + half:n_head], w[:, o_kr:o_kr + half], zpad],
                               axis=1)
        head_ref[...] = head.astype(BF16)
        tail_ref[...] = w[:, n_head:].astype(BF16)

    head, tail = pl.pallas_call(
        relayout,
        grid=(depth, D // bm),
        in_specs=[pl.BlockSpec((None, bm, n_cols), lambda l, i: (l, i, 0))],
        out_specs=[pl.BlockSpec((None, bm, n_head_p), lambda l, i: (l, i, 0)),
                   pl.BlockSpec((None, bm, n_cols - n_head), lambda l, i: (l, i, 0))],
        out_shape=[jax.ShapeDtypeStruct((depth, D, n_head_p), BF16),
                   jax.ShapeDtypeStruct((depth, D, n_cols - n_head), BF16)],
        compiler_params=_params("parallel", "parallel"),
        name="w_in_relayout",
    )(w_in)
    cols = dict(a=0, g=C, cq=2 * C, ckv=2 * C + q_lora, kr=o_kr, krs=o_kr + LANES,
                xm=0, z=D_ml, gate=2 * D_ml)
    return head, tail, cols


def kernel(x, mem, positions, norm_mix, w_in, b_gate, conv_dw, conv_dw_b, conv_ln_g, conv_ln_b, w_conv_out, mla_q_norm, mla_kv_norm, w_q_up, w_kv_up, mla_g_q, mla_g_k, w_mla_out, mlstm_conv_w, mlstm_conv_b, w_mq, w_mk, w_mv, w_if, b_if, mlstm_gn_g, mlstm_skip, w_mlstm_out, w_mix_out, norm_x, norm_mem, w_xq, w_xkv, xattn_g_q, xattn_g_k, w_xo, norm_ffn, w_ffn_in, w_ffn_out):
    B, S, D = x.shape
    T = B * S
    depth = w_in.shape[0]
    C = conv_dw.shape[2]
    D_ml = mlstm_conv_w.shape[2]
    xt = x.reshape(T, D)
    mem2d = mem.reshape(-1, D)
    cs1, cs2 = _rope_tables(positions.reshape(T, 1))
    w_head, w_tail, cols = _layout_w_in(w_in, C, w_q_up.shape[1], w_kv_up.shape[1], D_ml)
    w_conv_out, w_mla_out, w_mlstm_out, w_mix_out, w_ffn_in, w_ffn_out = (
        w.astype(BF16) for w in (w_conv_out, w_mla_out, w_mlstm_out, w_mix_out, w_ffn_in, w_ffn_out))
    h = _rmsnorm(xt, norm_mix[0])
    for l in range(depth):
        L = min(MLSTM_CHUNK, S)
        ph = _matmul(h, w_head, l, bm=2048, bn=1024, name="in_proj_head")
        pt = _matmul(h, w_tail, l, bm=2048, bn=1024, name="in_proj_tail")
        c = _conv_branch(ph, cols["a"], cols["g"], C, B, S, conv_dw[l], conv_dw_b[l],
                         conv_ln_g[l], conv_ln_b[l])
        q, k, v = _mla_prep(ph, (cols["cq"], cols["ckv"], cols["kr"], cols["krs"]), cs1, cs2,
                            mla_q_norm[l], mla_kv_norm[l], w_q_up[l], w_kv_up[l], mla_g_q[l], mla_g_k[l])
        a = _flash_attention(q, k, v, B, S)
        mq, mk, mv, xc, gc, gr = _mlstm_prep(pt, cols["xm"], B, S, mlstm_conv_w[l], mlstm_conv_b[l],
                                             w_mq[l], w_mk[l], w_mv[l], w_if[l], b_if[l], L=L)
        m = _mlstm_scan(mq, mk, mv, xc, pt, cols["z"], gc, gr, mlstm_gn_g[l], mlstm_skip[l], B, S, L=L)
        xt = _merge_mix(c, a, m, pt, cols["gate"], b_gate[l], w_conv_out, w_mla_out,
                        w_mlstm_out, w_mix_out, l, xt)
        xk, xv = _mem_kv(mem2d, norm_mem[l], w_xkv[l], xattn_g_k[l])
        xt = _xattn(xt, norm_x[l], w_xq[l], xattn_g_q[l], xk, xv, w_xo[l], B, S)
        xt, h = _ffn(xt, norm_ffn[l], w_ffn_in, w_ffn_out, l,
                     next_gain=norm_mix[l + 1] if l + 1 < depth else None)
    return xt.reshape(B, S, D)
```

```python
import functools

import jax
import jax.numpy as jnp
import numpy as np
from jax import lax
from jax.experimental import pallas as pl
from jax.experimental.pallas import tpu as pltpu

F32 = jnp.float32
BF16 = jnp.bfloat16

EPS = 1e-6
CONV_WIDTH = 31
MLA_HEADS = 8
NOPE_DIM = 128
ROPE_DIM = 64
V_DIM = 128
ROPE_BASE = 10000.0
MLSTM_HEADS = 4
MLSTM_CONV_WIDTH = 4
XATTN_HEADS = 4
XATTN_HEAD_DIM = 128
LANES = 128
SUBLANES = 8
NEG_BIG = -1e30
VMEM_LIMIT = 60 * 1024 * 1024

MLSTM_CHUNK = 512
CONV_HALO = 128
CONV_LOOKBACK = 32
MLSTM_HALO = 16


def _params(*sem):
    return pltpu.CompilerParams(dimension_semantics=sem, vmem_limit_bytes=VMEM_LIMIT)


def _sigmoid(x):
    return 0.5 * jnp.tanh(0.5 * x) + 0.5


def _silu(x):
    return x * _sigmoid(x)


def _dot(a, b):
    return jnp.dot(a, b, preferred_element_type=F32)


def _dot_nt(a, b):
    return lax.dot_general(a, b, (((1,), (1,)), ((), ())), preferred_element_type=F32)


def _rms(x, n):
    ms = jnp.sum(x * x, axis=-1, keepdims=True) * (1.0 / n)
    return x * lax.rsqrt(ms + EPS)


def _mm_kernel(x_ref, w_ref, o_ref):
    o_ref[...] = _dot_nt(x_ref[...], w_ref[...]).astype(o_ref.dtype)


def _matmul(x, wt, layer, *, bm, bn, name):
    M, K = x.shape
    N = wt.shape[1]
    bm, bn = min(bm, M), min(bn, N)
    assert M % bm == 0 and N % bn == 0 and wt.shape[2] == K
    w = wt
    return pl.pallas_call(
        _mm_kernel,
        grid=(M // bm, N // bn),
        in_specs=[pl.BlockSpec((bm, K), lambda i, j: (i, 0)),
                  pl.BlockSpec((None, bn, K), lambda i, j: (layer, j, 0))],
        out_specs=pl.BlockSpec((bm, bn), lambda i, j: (i, j)),
        out_shape=jax.ShapeDtypeStruct((M, N), BF16),
        compiler_params=_params("parallel", "arbitrary"),
        name=name,
    )(x, w)


def _conv_kernel(a_ref, g_ref, p_ref, w_ref, b_ref, lg_ref, lb_ref, o_ref, u_ref, *, bt, rt):
    i = pl.program_id(1)

    @pl.when(i == 0)
    def _():
        u_ref[0:CONV_HALO, :] = jnp.zeros((CONV_HALO, u_ref.shape[1]), BF16)

    @pl.when(i > 0)
    def _():
        u_ref[0:CONV_HALO, :] = u_ref[bt:bt + CONV_HALO, :]

    a = a_ref[...].astype(F32)
    g = g_ref[...].astype(F32)
    u_ref[CONV_HALO:CONV_HALO + bt, :] = (a * _sigmoid(g)).astype(BF16)
    bias = b_ref[...]
    sb = p_ref.shape[1] - CONV_LOOKBACK
    for k0 in range(0, bt, sb):
        u_win = u_ref[k0:k0 + sb + CONV_HALO, :]
        sh = [u_win[CONV_HALO - CONV_LOOKBACK:, :].astype(F32)]
        sh += [_dot(p_ref[r - 1], u_win) for r in range(1, SUBLANES)]
        for r0 in range(0, sb, rt):
            acc = [jnp.broadcast_to(bias, (SUBLANES, bias.shape[1]))] * (rt // SUBLANES)
            for j in range(CONV_WIDTH):
                shift = CONV_WIDTH - 1 - j
                r, start = shift % SUBLANES, r0 + CONV_LOOKBACK - (shift // SUBLANES) * SUBLANES
                wj = w_ref[j]
                acc = [acc[q] + wj * sh[r][start + q * SUBLANES:start + (q + 1) * SUBLANES, :]
                       for q in range(rt // SUBLANES)]
            acc = jnp.concatenate(acc, axis=0)
            mu = jnp.mean(acc, axis=-1, keepdims=True)
            d = acc - mu
            var = jnp.mean(d * d, axis=-1, keepdims=True)
            y = d * lax.rsqrt(var + EPS) * lg_ref[...] + lb_ref[...]
            o_ref[k0 + r0:k0 + r0 + rt, :] = _silu(y).astype(o_ref.dtype)


def _shift_matrices(sb):
    m = np.arange(sb + CONV_LOOKBACK)[None, :, None]
    k = np.arange(sb + CONV_HALO)[None, None, :]
    r = np.arange(1, SUBLANES)[:, None, None]
    return jnp.asarray(k == CONV_HALO - CONV_LOOKBACK + m - r, dtype=BF16)


def _conv_branch(proj, col_a, col_g, C, B, S, w, b, ln_g, ln_b, *, bt=256, sb=128, rt=32):
    bt = min(bt, S)
    assert bt >= CONV_HALO and bt % sb == 0 and sb % rt == 0
    nt = S // bt
    ca, cg = col_a // C, col_g // C
    pmat = _shift_matrices(sb)
    w = jnp.broadcast_to(w[:, None, :], (CONV_WIDTH, SUBLANES, C))
    const = lambda b_, i: (0, 0)
    return pl.pallas_call(
        functools.partial(_conv_kernel, bt=bt, rt=rt),
        grid=(B, nt),
        in_specs=[
            pl.BlockSpec((bt, C), lambda b_, i: (b_ * nt + i, ca)),
            pl.BlockSpec((bt, C), lambda b_, i: (b_ * nt + i, cg)),
            pl.BlockSpec(pmat.shape, lambda b_, i: (0, 0, 0)),
            pl.BlockSpec((CONV_WIDTH, SUBLANES, C), lambda b_, i: (0, 0, 0)),
            pl.BlockSpec((1, C), const),
            pl.BlockSpec((1, C), const),
            pl.BlockSpec((1, C), const),
        ],
        out_specs=pl.BlockSpec((bt, C), lambda b_, i: (b_ * nt + i, 0)),
        out_shape=jax.ShapeDtypeStruct((B * S, C), BF16),
        scratch_shapes=[pltpu.VMEM((CONV_HALO + bt, C), BF16)],
        compiler_params=_params("parallel", "arbitrary"),
        name="conformer_conv",
    )(proj, proj, pmat, w, b.reshape(1, C), ln_g.reshape(1, C), ln_b.reshape(1, C))


def _rope_kernel(pos_ref, freq_ref, sgn_ref, cs1_ref, cs2_ref):
    ang = pos_ref[...].astype(F32) * freq_ref[...]
    cs1_ref[...] = jnp.cos(ang)
    cs2_ref[...] = jnp.sin(ang) * sgn_ref[...]


def _rope_tables(pos, *, bm=1024):
    T = pos.shape[0]
    bm = min(bm, T)
    half = ROPE_DIM // 2
    inv_freq = ROPE_BASE ** (-jnp.arange(half, dtype=F32) / half)
    freq = _pad_lanes(jnp.concatenate([inv_freq, inv_freq]))[None, :]
    sgn = _pad_lanes(jnp.concatenate([-jnp.ones((half,), F32), jnp.ones((half,), F32)]))[None, :]
    const = lambda i: (0, 0)
    tab = jax.ShapeDtypeStruct((T, LANES), F32)
    return pl.pallas_call(
        _rope_kernel,
        grid=(T // bm,),
        in_specs=[pl.BlockSpec((bm, 1), lambda i: (i, 0)), pl.BlockSpec((1, LANES), const),
                  pl.BlockSpec((1, LANES), const)],
        out_specs=[pl.BlockSpec((bm, LANES), lambda i: (i, 0)), pl.BlockSpec((bm, LANES), lambda i: (i, 0))],
        out_shape=[tab, tab],
        compiler_params=_params("parallel"),
        name="rope_tables",
    )(pos, freq, sgn)


def _mla_prep_kernel(cq_ref, ckv_ref, kr_ref, krs_ref, cs1_ref, cs2_ref,
                     qn_ref, kvn_ref, wq_ref, wkv_ref, gq_ref, gk_ref,
                     q_ref, k_ref, v_ref, *, scale):
    H = MLA_HEADS
    cs1 = cs1_ref[...]
    cs2 = cs2_ref[...]

    cq = cq_ref[...].astype(F32)
    hq = (_rms(cq, cq.shape[-1]) * qn_ref[...]).astype(BF16)
    qall = _dot(hq, wq_ref[...])
    ckv = ckv_ref[...].astype(F32)
    hkv = (_rms(ckv, ckv.shape[-1]) * kvn_ref[...]).astype(BF16)
    kvall = _dot(hkv, wkv_ref[...])

    g_nope_q, g_rot_q, g_rots_q = gq_ref[0:1, :], gq_ref[1:2, :], gq_ref[2:3, :]
    g_nope_k, g_rot_k, g_rots_k = gk_ref[0:1, :], gk_ref[1:2, :], gk_ref[2:3, :]

    kr = kr_ref[...].astype(F32)
    krs = krs_ref[...].astype(F32)
    rs = lax.rsqrt(jnp.sum(kr * kr, axis=-1, keepdims=True) * (1.0 / ROPE_DIM) + EPS)
    k_rot = ((kr * rs * g_rot_k) * cs1 + (krs * rs * g_rots_k) * cs2).astype(BF16)

    for h in range(H):
        qn = qall[:, h * LANES:(h + 1) * LANES]
        q_ref[h, :, 0:LANES] = (_rms(qn, NOPE_DIM) * g_nope_q * scale).astype(BF16)
        qr = qall[:, (H + h) * LANES:(H + h + 1) * LANES]
        qs = qall[:, (2 * H + h) * LANES:(2 * H + h + 1) * LANES]
        rq = lax.rsqrt(jnp.sum(qr * qr, axis=-1, keepdims=True) * (1.0 / ROPE_DIM) + EPS)
        q_rot = (qr * rq * g_rot_q) * cs1 + (qs * rq * g_rots_q) * cs2
        q_ref[h, :, LANES:2 * LANES] = (q_rot * scale).astype(BF16)
        kn = kvall[:, h * LANES:(h + 1) * LANES]
        k_ref[h, :, 0:LANES] = (_rms(kn, NOPE_DIM) * g_nope_k).astype(BF16)
        k_ref[h, :, LANES:2 * LANES] = k_rot
        v_ref[h, :, 0:V_DIM] = kvall[:, (H + h) * LANES:(H + h + 1) * LANES].astype(BF16)
        v_ref[h, :, V_DIM:V_DIM + LANES] = jnp.ones((kvall.shape[0], LANES), BF16)


def _pad_lanes(v, n=LANES):
    return jnp.pad(v, [(0, 0)] * (v.ndim - 1) + [(0, n - v.shape[-1])])


def _swap_halves(v):
    half = v.shape[-1] // 2
    return jnp.concatenate([v[..., half:], v[..., :half]], axis=-1)


def _mla_prep(proj, cols, cs1, cs2, mla_q_norm, mla_kv_norm, w_q_up, w_kv_up, g_q, g_k, *, bm=512):
    T = proj.shape[0]
    bm = min(bm, T)
    H = MLA_HEADS
    q_lora, kv_lora = w_q_up.shape[0], w_kv_up.shape[0]
    c_cq, c_ckv, c_kr, c_krs = cols
    wq = w_q_up.reshape(q_lora, H, NOPE_DIM + ROPE_DIM)
    wq_rot = wq[:, :, NOPE_DIM:]
    wq_p = jnp.concatenate([
        wq[:, :, :NOPE_DIM].reshape(q_lora, H * NOPE_DIM),
        _pad_lanes(wq_rot).reshape(q_lora, H * LANES),
        _pad_lanes(_swap_halves(wq_rot)).reshape(q_lora, H * LANES)], axis=1).astype(BF16)
    wkv = w_kv_up.reshape(kv_lora, H, NOPE_DIM + V_DIM)
    wkv_p = jnp.concatenate([wkv[:, :, :NOPE_DIM].reshape(kv_lora, H * NOPE_DIM),
                             wkv[:, :, NOPE_DIM:].reshape(kv_lora, H * V_DIM)], axis=1).astype(BF16)

    def gains(g):
        return jnp.stack([g[:NOPE_DIM], _pad_lanes(g[NOPE_DIM:]), _pad_lanes(_swap_halves(g[NOPE_DIM:]))])

    scale = float((NOPE_DIM + ROPE_DIM) ** -0.5 * np.log2(np.e))
    const = lambda i: (0, 0)
    return pl.pallas_call(
        functools.partial(_mla_prep_kernel, scale=scale),
        grid=(T // bm,),
        in_specs=[
            pl.BlockSpec((bm, q_lora), lambda i: (i, c_cq // q_lora)),
            pl.BlockSpec((bm, kv_lora), lambda i: (i, c_ckv // kv_lora)),
            pl.BlockSpec((bm, LANES), lambda i: (i, c_kr // LANES)),
            pl.BlockSpec((bm, LANES), lambda i: (i, c_krs // LANES)),
            pl.BlockSpec((bm, LANES), lambda i: (i, 0)),
            pl.BlockSpec((bm, LANES), lambda i: (i, 0)),
            pl.BlockSpec((1, q_lora), const),
            pl.BlockSpec((1, kv_lora), const),
            pl.BlockSpec(wq_p.shape, const),
            pl.BlockSpec(wkv_p.shape, const),
            pl.BlockSpec((3, LANES), const),
            pl.BlockSpec((3, LANES), const),
        ],
        out_specs=[
            pl.BlockSpec((H, bm, 2 * LANES), lambda i: (0, i, 0)),
            pl.BlockSpec((H, bm, 2 * LANES), lambda i: (0, i, 0)),
            pl.BlockSpec((H, bm, V_DIM + LANES), lambda i: (0, i, 0)),
        ],
        out_shape=[
            jax.ShapeDtypeStruct((H, T, 2 * LANES), BF16),
            jax.ShapeDtypeStruct((H, T, 2 * LANES), BF16),
            jax.ShapeDtypeStruct((H, T, V_DIM + LANES), BF16),
        ],
        compiler_params=_params("parallel"),
        name="mla_prep",
    )(proj, proj, proj, proj, cs1, cs2, mla_q_norm.reshape(1, -1), mla_kv_norm.reshape(1, -1),
      wq_p, wkv_p, gains(g_q), gains(g_k))


def _flash_kernel(q_ref, k_ref, v_ref, o_ref, m_ref, acc_ref, *, bh, nh):
    i = pl.program_id(2)
    m_ref[...] = jnp.full(m_ref.shape, NEG_BIG, F32)
    acc_ref[...] = jnp.zeros(acc_ref.shape, F32)
    nc = bh // LANES

    def chain(a, j, masked):
        rows = slice(a * bh, (a + 1) * bh)
        start = pl.multiple_of(j * bh, bh)
        k = k_ref[0, pl.ds(start, bh), :]
        v = v_ref[0, pl.ds(start, bh), :]
        s = _dot_nt(q_ref[0, rows, :], k)
        if masked:
            row = lax.broadcasted_iota(jnp.int32, s.shape, 0)
            col = lax.broadcasted_iota(jnp.int32, s.shape, 1)
            s = jnp.where(col <= row, s, NEG_BIG)
        chunks = [s[:, c * LANES:(c + 1) * LANES] for c in range(nc)]
        mx = functools.reduce(jnp.maximum, chunks)
        m_prev = m_ref[rows, :]
        m_new = jnp.maximum(m_prev, jnp.max(mx, axis=-1, keepdims=True))
        alpha = jnp.exp2(m_prev - m_new)
        p = jnp.concatenate([jnp.exp2(c - m_new) for c in chunks], axis=1).astype(BF16)
        alpha2 = jnp.concatenate([alpha] * (acc_ref.shape[1] // LANES), axis=1)
        acc_ref[rows, :] = alpha2 * acc_ref[rows, :] + _dot(p, v)
        m_ref[rows, :] = m_new

    def body(j, carry):
        for a in range(nh):
            chain(a, j, False)
        return carry

    lax.fori_loop(0, i * nh, body, 0)
    for a in range(nh):
        for c in range(a + 1):
            chain(a, i * nh + c, c == a)
    o_ref[...] = (acc_ref[:, 0:V_DIM] / acc_ref[:, V_DIM:2 * V_DIM]).astype(o_ref.dtype)


def _flash_attention(q, k, v, B, S, *, bh=512, nh=8):
    H = q.shape[0]
    bh = min(bh, S // nh)
    bq = bh * nh
    nq = S // bq
    return pl.pallas_call(
        functools.partial(_flash_kernel, bh=bh, nh=nh),
        grid=(B, H, nq),
        in_specs=[
            pl.BlockSpec((1, bq, q.shape[2]), lambda b, h, i: (h, b * nq + i, 0)),
            pl.BlockSpec((1, S, k.shape[2]), lambda b, h, i: (h, b, 0)),
            pl.BlockSpec((1, S, v.shape[2]), lambda b, h, i: (h, b, 0)),
        ],
        out_specs=pl.BlockSpec((bq, V_DIM), lambda b, h, i: (b * nq + i, h)),
        out_shape=jax.ShapeDtypeStruct((B * S, H * V_DIM), BF16),
        scratch_shapes=[pltpu.VMEM((bq, LANES), F32), pltpu.VMEM((bq, v.shape[2]), F32)],
        compiler_params=_params("parallel", "parallel", "arbitrary"),
        name="mla_flash",
    )(q, k, v)


def _log_sigmoid(x):
    return -(jnp.maximum(-x, 0.0) + jnp.log(1.0 + jnp.exp(-jnp.abs(x))))


def _split_dot(a, b_exact):
    hi = a.astype(BF16)
    lo = (a - hi.astype(F32)).astype(BF16)
    return _dot(hi, b_exact) + _dot(lo, b_exact)


def _mlstm_prep_kernel(xm_ref, halo_ref, cw_ref, cb_ref, wq_ref, wk_ref, wv_ref, wif_ref, wift_ref,
                       bif_ref, bift_ref, q_ref, k_ref, v_ref, xc_ref, gc_ref, gr_ref,
                       buf_ref, qkv_ref, *, bm, L):
    i = pl.program_id(1)
    H = MLSTM_HEADS
    D = xm_ref.shape[1]
    DH = D // H
    halo = halo_ref[...].astype(F32)
    buf_ref[0:MLSTM_HALO, :] = jnp.where(i > 0, halo, 0.0)
    xm_bf = xm_ref[...]
    buf_ref[MLSTM_HALO:MLSTM_HALO + bm, :] = xm_bf.astype(F32)
    off = MLSTM_HALO - (MLSTM_CONV_WIDTH - 1)
    acc = jnp.broadcast_to(cb_ref[...], (bm, D))
    for j in range(MLSTM_CONV_WIDTH):
        acc = acc + cw_ref[j:j + 1, :] * buf_ref[off + j:off + j + bm, :]
    xc = _silu(acc)
    xc_bf = xc.astype(BF16)
    xc_ref[...] = xc_bf
    for h in range(H):
        sl = slice(h * DH, (h + 1) * DH)
        qh = _dot(xc_bf[:, sl], wq_ref[h]).astype(BF16)
        k_raw = _dot(xc_bf[:, sl], wk_ref[h])
        vh = _dot(xm_bf[:, sl], wv_ref[h]).astype(BF16)
        q_ref[:, sl] = qh
        k_ref[:, sl] = (k_raw * (DH ** -0.5)).astype(BF16)
        v_ref[:, sl] = vh
        qkv_ref[:, h * DH:(h + 1) * DH] = qh
        qkv_ref[:, D + h * DH:D + (h + 1) * DH] = k_raw.astype(BF16)
        qkv_ref[:, 2 * D + h * DH:2 * D + (h + 1) * DH] = vh
    qkv = qkv_ref[...]
    pre_c = _dot(qkv, wif_ref[...]) + bif_ref[...]
    lane = lax.broadcasted_iota(jnp.int32, pre_c.shape, 1)
    is_f_c = (lane >= H) & (lane < 2 * H)
    lf_c = jnp.where(is_f_c, _log_sigmoid(pre_c), 0.0)
    r = lax.broadcasted_iota(jnp.int32, (bm, bm), 0)
    c = lax.broadcasted_iota(jnp.int32, (bm, bm), 1)
    same = (r // L) == (c // L)
    tri = jnp.where(same & (c <= r), 1.0, 0.0).astype(BF16)
    cum_c = _split_dot_left(tri, lf_c)
    gc_ref[...] = jnp.where(is_f_c, cum_c, pre_c)
    pre_r = _dot_nt(wift_ref[...], qkv) + bift_ref[...]
    row = lax.broadcasted_iota(jnp.int32, pre_r.shape, 0)
    is_f_r = row >= H
    lf_r = jnp.where(is_f_r, _log_sigmoid(pre_r), 0.0)
    trit = jnp.where(same & (r <= c), 1.0, 0.0).astype(BF16)
    cum_r = _split_dot(lf_r, trit)
    gr_ref[...] = jnp.where(is_f_r, cum_r, pre_r)


def _split_dot_left(a_exact, b):
    hi = b.astype(BF16)
    lo = (b - hi.astype(F32)).astype(BF16)
    return _dot(a_exact, hi) + _dot(a_exact, lo)


def _mlstm_prep(proj, col_xm, B, S, conv_w, conv_b, w_mq, w_mk, w_mv, w_if, b_if, *, bm=512, L):
    T = B * S
    H = MLSTM_HEADS
    D = conv_w.shape[1]
    bm = min(bm, S)
    assert bm % L == 0
    nt = S // bm
    cxm = col_xm // D
    hb = bm // MLSTM_HALO
    wif = _pad_lanes(w_if).astype(BF16)
    wift = w_if.T.astype(BF16)
    bif = _pad_lanes(b_if)[None, :]
    bift = b_if[:, None]
    const2 = lambda b_, i: (0, 0)
    const3 = lambda b_, i: (0, 0, 0)
    row_blk = lambda b_, i: (b_ * nt + i, 0)
    act = jax.ShapeDtypeStruct((T, D), BF16)
    return pl.pallas_call(
        functools.partial(_mlstm_prep_kernel, bm=bm, L=L),
        grid=(B, nt),
        in_specs=[
            pl.BlockSpec((bm, D), lambda b_, i: (b_ * nt + i, cxm)),
            pl.BlockSpec((MLSTM_HALO, D), lambda b_, i: (jnp.maximum((b_ * nt + i) * hb - 1, 0), cxm)),
            pl.BlockSpec((MLSTM_CONV_WIDTH, D), const2),
            pl.BlockSpec((1, D), const2),
            pl.BlockSpec(w_mq.shape, const3),
            pl.BlockSpec(w_mk.shape, const3),
            pl.BlockSpec(w_mv.shape, const3),
            pl.BlockSpec(wif.shape, const2),
            pl.BlockSpec(wift.shape, const2),
            pl.BlockSpec((1, LANES), const2),
            pl.BlockSpec((2 * H, 1), const2),
        ],
        out_specs=[
            pl.BlockSpec((bm, D), row_blk), pl.BlockSpec((bm, D), row_blk),
            pl.BlockSpec((bm, D), row_blk), pl.BlockSpec((bm, D), row_blk),
            pl.BlockSpec((bm, LANES), row_blk),
            pl.BlockSpec((2 * H, bm), lambda b_, i: (0, b_ * nt + i)),
        ],
        out_shape=[act, act, act, act,
                   jax.ShapeDtypeStruct((T, LANES), F32),
                   jax.ShapeDtypeStruct((2 * H, T), F32)],
        scratch_shapes=[pltpu.VMEM((MLSTM_HALO + bm, D), F32), pltpu.VMEM((bm, 3 * D), BF16)],
        compiler_params=_params("parallel", "arbitrary"),
        name="mlstm_prep",
    )(proj, proj, conv_w, conv_b.reshape(1, D), w_mq.astype(BF16), w_mk.astype(BF16),
      w_mv.astype(BF16), wif, wift, bif, bift)


def _mlstm_scan_kernel(q_ref, k_ref, v_ref, xc_ref, z_ref, gc_ref, gr_ref, gn_ref, skip_ref,
                       o_ref, c_ref, n_ref, m_ref, *, L):
    H = MLSTM_HEADS
    DH = q_ref.shape[1] // H

    @pl.when(pl.program_id(1) == 0)
    def _():
        c_ref[...] = jnp.zeros(c_ref.shape, F32)
        n_ref[...] = jnp.zeros(n_ref.shape, F32)
        m_ref[...] = jnp.zeros(m_ref.shape, F32)

    row = lax.broadcasted_iota(jnp.int32, (L, L), 0)
    col = lax.broadcasted_iota(jnp.int32, (L, L), 1)
    causal = col <= row
    for cc, h in [(cc, h) for cc in range(q_ref.shape[0] // L) for h in range(H)]:
        rs = slice(cc * L, (cc + 1) * L)
        sl = slice(h * DH, (h + 1) * DH)
        q = q_ref[rs, sl]
        k = k_ref[rs, sl]
        v = v_ref[rs, sl]
        li_c = gc_ref[rs, h:h + 1]
        b_c = gc_ref[rs, H + h:H + h + 1]
        li_r = gr_ref[h:h + 1, rs]
        b_r = gr_ref[H + h:H + h + 1, rs]
        m_prev = m_ref[h, 0:1, 0:1]
        C = c_ref[h]
        n = n_ref[h]

        dlog = jnp.where(causal, b_c - b_r + li_r, NEG_BIG)
        inter = b_c + m_prev
        m_comb = jnp.maximum(inter, jnp.max(dlog, axis=-1, keepdims=True))
        w_intra = jnp.exp(dlog - m_comb)
        w_inter = jnp.exp(inter - m_comb)
        s = _dot_nt(q, k) * w_intra
        num = w_inter * _dot(q, C.astype(BF16)) + _dot(s.astype(BF16), v)
        qf = q.astype(F32)
        den = w_inter * jnp.sum(qf * n, axis=-1, keepdims=True) + jnp.sum(s, axis=-1, keepdims=True)
        hh = num / jnp.maximum(jnp.abs(den), jnp.exp(-m_comb))

        b_last = b_c[L - 1:L, :]
        g_c = b_last - b_c + li_c
        m_new = jnp.maximum(b_last + m_prev, jnp.max(g_c, axis=0, keepdims=True))
        wg_c = jnp.exp(g_c - m_new)
        decay = jnp.exp(b_last + m_prev - m_new)
        kw = k.astype(F32) * wg_c
        c_ref[h] = decay * C + _dot(kw.T.astype(BF16), v)
        n_ref[h] = decay * n + jnp.sum(kw, axis=0, keepdims=True)
        m_ref[h] = jnp.broadcast_to(m_new, m_ref.shape[1:])

        mu = jnp.mean(hh, axis=-1, keepdims=True)
        d = hh - mu
        var = jnp.mean(d * d, axis=-1, keepdims=True)
        hn = d * lax.rsqrt(var + EPS) * gn_ref[:, sl]
        hn = hn + skip_ref[:, sl] * xc_ref[rs, sl].astype(F32)
        o_ref[rs, sl] = (hn * _silu(z_ref[rs, sl].astype(F32))).astype(o_ref.dtype)


def _mlstm_scan(q, k, v, xc, proj, col_z, gc, gr, gn_g, skip, B, S, *, L, chunks=2):
    T, D = q.shape
    H = MLSTM_HEADS
    DH = D // H
    R = L * min(chunks, S // L)
    nc = S // R
    cz = col_z // D
    row_blk = lambda b_, i: (b_ * nc + i, 0)
    const2 = lambda b_, i: (0, 0)
    return pl.pallas_call(
        functools.partial(_mlstm_scan_kernel, L=L),
        grid=(B, nc),
        in_specs=[
            pl.BlockSpec((R, D), row_blk), pl.BlockSpec((R, D), row_blk),
            pl.BlockSpec((R, D), row_blk), pl.BlockSpec((R, D), row_blk),
            pl.BlockSpec((R, D), lambda b_, i: (b_ * nc + i, cz)),
            pl.BlockSpec((R, LANES), row_blk),
            pl.BlockSpec((2 * H, R), lambda b_, i: (0, b_ * nc + i)),
            pl.BlockSpec((1, D), const2),
            pl.BlockSpec((1, D), const2),
        ],
        out_specs=pl.BlockSpec((R, D), row_blk),
        out_shape=jax.ShapeDtypeStruct((T, D), BF16),
        scratch_shapes=[pltpu.VMEM((H, DH, DH), F32), pltpu.VMEM((H, 1, DH), F32),
                        pltpu.VMEM((H, 8, LANES), F32)],
        compiler_params=_params("parallel", "arbitrary"),
        name="mlstm_scan",
    )(q, k, v, xc, proj, gc, gr, gn_g.reshape(1, D), skip.reshape(1, D))


def _merge_mix_kernel(c_ref, a_ref, m_ref, g0_ref, g1_ref, g2_ref, bg_ref, wc_ref, wa_ref, wm_ref,
                      wmix_ref, x_ref, o_ref):
    N = x_ref.shape[1]

    def gate(t, g_ref):
        return _sigmoid(g_ref[...].astype(F32) + bg_ref[:, t * N:(t + 1) * N])

    y = gate(0, g0_ref) * _dot(c_ref[...], wc_ref[...])
    y = y + gate(1, g1_ref) * _dot(a_ref[...], wa_ref[...])
    y = y + gate(2, g2_ref) * _dot(m_ref[...], wm_ref[...])
    o_ref[...] = x_ref[...] + _dot(y.astype(BF16), wmix_ref[...])


def _layer_spec(w, layer, **kw):
    return pl.BlockSpec((None,) + w.shape[1:], lambda *idx: (layer, 0, 0), **kw)


def _merge_mix(c, a, m, proj, col_gate, b_gate, wc, wa, wm, wmix, layer, x, *, bm=256):
    T, K = c.shape
    N = wc.shape[2]
    bm = min(bm, T)
    assert col_gate % N == 0
    gcb = col_gate // N
    row = lambda i: (i, 0)
    xs = pl.BlockSpec((bm, K), row)
    once = dict(pipeline_mode=pl.Buffered(1))
    return pl.pallas_call(
        _merge_mix_kernel,
        grid=(T // bm,),
        in_specs=[xs, xs, xs,
                  pl.BlockSpec((bm, N), lambda i: (i, gcb)),
                  pl.BlockSpec((bm, N), lambda i: (i, gcb + 1)),
                  pl.BlockSpec((bm, N), lambda i: (i, gcb + 2)),
                  pl.BlockSpec((1, 3 * N), lambda i: (0, 0)),
                  _layer_spec(wc, layer, **once), _layer_spec(wa, layer, **once),
                  _layer_spec(wm, layer, **once), _layer_spec(wmix, layer, **once),
                  pl.BlockSpec((bm, N), row)],
        out_specs=pl.BlockSpec((bm, N), row),
        out_shape=jax.ShapeDtypeStruct((T, N), F32),
        compiler_params=_params("parallel"),
        name="merge_mix",
    )(c, a, m, proj, proj, proj, b_gate.reshape(1, 3 * N), wc, wa, wm, wmix, x)


def _ffn_kernel(x_ref, g_ref, gn_ref, wg_ref, wu_ref, wo_ref, o_ref, h_ref, *, emit_next):
    j = pl.program_id(1)

    @pl.when(j == 0)
    def _():
        xf = x_ref[...]
        h_ref[...] = (_rms(xf, xf.shape[-1]) * g_ref[...]).astype(BF16)
        o_ref[...] = xf

    h = h_ref[...]
    y = (_silu(_dot(h, wg_ref[...])) * _dot(h, wu_ref[...])).astype(BF16)
    o_ref[...] += _dot(y, wo_ref[...])

    if emit_next:
        @pl.when(j == pl.num_programs(1) - 1)
        def _():
            of = o_ref[...]
            h_ref[...] = (_rms(of, of.shape[-1]) * gn_ref[...]).astype(BF16)


def _ffn(x, gain, w_in, w_out, layer, next_gain=None, *, bm=1024, bh=512):
    T, D = x.shape
    F = w_out.shape[1]
    bm, bh = min(bm, T), min(bh, F)
    nh = F // bh
    emit = next_gain is not None
    gn = (next_gain if emit else gain).reshape(1, D).astype(F32)
    row = pl.BlockSpec((bm, D), lambda i, j: (i, 0))
    vec = pl.BlockSpec((1, D), lambda i, j: (0, 0))
    out_shape = [jax.ShapeDtypeStruct((T, D), F32)]
    out_specs = [row]
    scratch = [pltpu.VMEM((bm, D), BF16)]
    if emit:
        out_shape.append(jax.ShapeDtypeStruct((T, D), BF16))
        out_specs.append(row)
        scratch = []
    res = pl.pallas_call(
        functools.partial(_ffn_kernel, emit_next=emit),
        grid=(T // bm, nh),
        in_specs=[row, vec, vec,
                  pl.BlockSpec((None, D, bh), lambda i, j: (layer, 0, j)),
                  pl.BlockSpec((None, D, bh), lambda i, j: (layer, 0, j + nh)),
                  pl.BlockSpec((None, bh, D), lambda i, j: (layer, j, 0))],
        out_specs=out_specs,
        out_shape=out_shape,
        scratch_shapes=scratch,
        compiler_params=_params("parallel", "arbitrary"),
        name="ffn",
    )(x, gain.reshape(1, D).astype(F32), gn, w_in, w_in, w_out)
    return (res[0], res[1]) if emit else (res[0], None)


def _norm_kernel(x_ref, g_ref, o_ref):
    xf = x_ref[...]
    o_ref[...] = (_rms(xf, xf.shape[-1]) * g_ref[...]).astype(o_ref.dtype)


def _rmsnorm(x, gain, *, bm=1024):
    T, D = x.shape
    bm = min(bm, T)
    return pl.pallas_call(
        _norm_kernel,
        grid=(T // bm,),
        in_specs=[pl.BlockSpec((bm, D), lambda i: (i, 0)), pl.BlockSpec((1, D), lambda i: (0, 0))],
        out_specs=pl.BlockSpec((bm, D), lambda i: (i, 0)),
        out_shape=jax.ShapeDtypeStruct((T, D), BF16),
        compiler_params=_params("parallel"),
        name="rmsnorm",
    )(x, gain.reshape(1, D).astype(F32))


def _mem_kv_kernel(mem_ref, g_ref, w_ref, gk_ref, k_ref, v_ref):
    H, DH = XATTN_HEADS, XATTN_HEAD_DIM
    x = mem_ref[...]
    h = (_rms(x, x.shape[-1]) * g_ref[...]).astype(BF16)
    kv = _dot(h, w_ref[...])
    for hd in range(H):
        kh = kv[:, hd * DH:(hd + 1) * DH]
        k_ref[:, hd * DH:(hd + 1) * DH] = (_rms(kh, DH) * gk_ref[...]).astype(BF16)
    v_ref[...] = kv[:, H * DH:].astype(BF16)


def _mem_kv(mem2d, norm_mem, w_xkv, g_k, *, bm=256):
    M, D = mem2d.shape
    HD = XATTN_HEADS * XATTN_HEAD_DIM
    bm = min(bm, M)
    const = lambda i: (0, 0)
    out = jax.ShapeDtypeStruct((M, HD), BF16)
    return pl.pallas_call(
        _mem_kv_kernel,
        grid=(M // bm,),
        in_specs=[pl.BlockSpec((bm, D), lambda i: (i, 0)), pl.BlockSpec((1, D), const),
                  pl.BlockSpec((D, 2 * HD), const), pl.BlockSpec((1, XATTN_HEAD_DIM), const)],
        out_specs=[pl.BlockSpec((bm, HD), lambda i: (i, 0)), pl.BlockSpec((bm, HD), lambda i: (i, 0))],
        out_shape=[out, out],
        compiler_params=_params("parallel"),
        name="xattn_mem_kv",
    )(mem2d, norm_mem.reshape(1, D), w_xkv.astype(BF16), g_k.reshape(1, -1))


def _xattn_kernel(x_ref, g_ref, wq_ref, gq_ref, k_ref, v_ref, wo_ref, o_ref, att_ref, *, scale):
    H, DH = XATTN_HEADS, XATTN_HEAD_DIM
    x = x_ref[...]
    h = (_rms(x, x.shape[-1]) * g_ref[...]).astype(BF16)
    q = _dot(h, wq_ref[...])
    for hd in range(H):
        sl = slice(hd * DH, (hd + 1) * DH)
        qh = (_rms(q[:, sl], DH) * gq_ref[...] * scale).astype(BF16)
        s = _dot_nt(qh, k_ref[:, sl])
        e = jnp.exp(s - jnp.max(s, axis=-1, keepdims=True))
        o = _dot(e.astype(BF16), v_ref[:, sl]) / jnp.sum(e, axis=-1, keepdims=True)
        att_ref[:, sl] = o.astype(BF16)
    o_ref[...] = x + _dot(att_ref[...], wo_ref[...])


def _xattn(x, norm_x, w_xq, g_q, k, v, w_xo, B, S, *, bm=512):
    T, D = x.shape
    HD = XATTN_HEADS * XATTN_HEAD_DIM
    M = k.shape[0] // B
    bm = min(bm, S)
    nt = S // bm
    const = lambda b_, i: (0, 0)
    return pl.pallas_call(
        functools.partial(_xattn_kernel, scale=float(XATTN_HEAD_DIM ** -0.5)),
        grid=(B, nt),
        in_specs=[
            pl.BlockSpec((bm, D), lambda b_, i: (b_ * nt + i, 0)),
            pl.BlockSpec((1, D), const),
            pl.BlockSpec((D, HD), const),
            pl.BlockSpec((1, XATTN_HEAD_DIM), const),
            pl.BlockSpec((M, HD), lambda b_, i: (b_, 0)),
            pl.BlockSpec((M, HD), lambda b_, i: (b_, 0)),
            pl.BlockSpec((HD, D), const),
        ],
        out_specs=pl.BlockSpec((bm, D), lambda b_, i: (b_ * nt + i, 0)),
        out_shape=jax.ShapeDtypeStruct((T, D), F32),
        scratch_shapes=[pltpu.VMEM((bm, HD), BF16)],
        compiler_params=_params("parallel", "arbitrary"),
        name="xattn",
    )(x, norm_x.reshape(1, D), w_xq.astype(BF16), g_q.reshape(1, -1), k, v, w_xo.astype(BF16))


def _layout_w_in(w_in, C, q_lora, kv_lora, D_ml):
    wt = jnp.swapaxes(w_in, 1, 2)
    n_head = 2 * C + q_lora + kv_lora + ROPE_DIM
    o_kr = n_head - ROPE_DIM
    half = ROPE_DIM // 2
    zpad = jnp.zeros((wt.shape[0], LANES - ROPE_DIM, wt.shape[2]), BF16)
    tail = wt[:, n_head:, :].astype(BF16)
    head = jnp.concatenate([wt[:, :n_head, :].astype(BF16), zpad, wt[:, o_kr + half:n_head, :].astype(BF16),
                            wt[:, o_kr:o_kr + half, :].astype(BF16), zpad], axis=1)
    cols = dict(a=0, g=C, cq=2 * C, ckv=2 * C + q_lora, kr=o_kr, krs=o_kr + LANES,
                xm=0, z=D_ml, gate=2 * D_ml)
    return head, tail, cols


def kernel(x, mem, positions, norm_mix, w_in, b_gate, conv_dw, conv_dw_b, conv_ln_g, conv_ln_b, w_conv_out, mla_q_norm, mla_kv_norm, w_q_up, w_kv_up, mla_g_q, mla_g_k, w_mla_out, mlstm_conv_w, mlstm_conv_b, w_mq, w_mk, w_mv, w_if, b_if, mlstm_gn_g, mlstm_skip, w_mlstm_out, w_mix_out, norm_x, norm_mem, w_xq, w_xkv, xattn_g_q, xattn_g_k, w_xo, norm_ffn, w_ffn_in, w_ffn_out):
    B, S, D = x.shape
    T = B * S
    depth = w_in.shape[0]
    C = conv_dw.shape[2]
    D_ml = mlstm_conv_w.shape[2]
    xt = x.reshape(T, D)
    mem2d = mem.reshape(-1, D)
    cs1, cs2 = _rope_tables(positions.reshape(T, 1))
    w_head, w_tail, cols = _layout_w_in(w_in, C, w_q_up.shape[1], w_kv_up.shape[1], D_ml)
    w_conv_out, w_mla_out, w_mlstm_out, w_mix_out, w_ffn_in, w_ffn_out = (
        w.astype(BF16) for w in (w_conv_out, w_mla_out, w_mlstm_out, w_mix_out, w_ffn_in, w_ffn_out))
    h = _rmsnorm(xt, norm_mix[0])
    for l in range(depth):
        L = min(MLSTM_CHUNK, S)
        ph = _matmul(h, w_head, l, bm=2048, bn=1024, name="in_proj_head")
        pt = _matmul(h, w_tail, l, bm=2048, bn=1024, name="in_proj_tail")
        c = _conv_branch(ph, cols["a"], cols["g"], C, B, S, conv_dw[l], conv_dw_b[l],
                         conv_ln_g[l], conv_ln_b[l])
        q, k, v = _mla_prep(ph, (cols["cq"], cols["ckv"], cols["kr"], cols["krs"]), cs1, cs2,
                            mla_q_norm[l], mla_kv_norm[l], w_q_up[l], w_kv_up[l], mla_g_q[l], mla_g_k[l])
        a = _flash_attention(q, k, v, B, S)
        mq, mk, mv, xc, gc, gr = _mlstm_prep(pt, cols["xm"], B, S, mlstm_conv_w[l], mlstm_conv_b[l],
                                             w_mq[l], w_mk[l], w_mv[l], w_if[l], b_if[l], L=L)
        m = _mlstm_scan(mq, mk, mv, xc, pt, cols["z"], gc, gr, mlstm_gn_g[l], mlstm_skip[l], B, S, L=L)
        xt = _merge_mix(c, a, m, pt, cols["gate"], b_gate[l], w_conv_out, w_mla_out,
                        w_mlstm_out, w_mix_out, l, xt)
        xk, xv = _mem_kv(mem2d, norm_mem[l], w_xkv[l], xattn_g_k[l])
        xt = _xattn(xt, norm_x[l], w_xq[l], xattn_g_q[l], xk, xv, w_xo[l], B, S)
        xt, h = _ffn(xt, norm_ffn[l], w_ffn_in, w_ffn_out, l,
                     next_gain=norm_mix[l + 1] if l + 1 < depth else None)
    return xt.reshape(B, S, D)
```

```python
import functools

import jax
import jax.numpy as jnp
import numpy as np
from jax import lax
from jax.experimental import pallas as pl
from jax.experimental.pallas import tpu as pltpu

F32 = jnp.float32
BF16 = jnp.bfloat16

EPS = 1e-6
CONV_WIDTH = 31
MLA_HEADS = 8
NOPE_DIM = 128
ROPE_DIM = 64
V_DIM = 128
ROPE_BASE = 10000.0
MLSTM_HEADS = 4
MLSTM_CONV_WIDTH = 4
XATTN_HEADS = 4
XATTN_HEAD_DIM = 128
LANES = 128
SUBLANES = 8
NEG_BIG = -1e30
VMEM_LIMIT = 60 * 1024 * 1024

MLSTM_CHUNK = 512
CONV_HALO = 128
CONV_LOOKBACK = 32
MLSTM_HALO = 16
MLSTM_PREP_PARTS = 2


def _params(*sem):
    return pltpu.CompilerParams(dimension_semantics=sem, vmem_limit_bytes=VMEM_LIMIT)


def _sigmoid(x):
    return 0.5 * jnp.tanh(0.5 * x) + 0.5


def _silu(x):
    return x * _sigmoid(x)


def _dot(a, b):
    return jnp.dot(a, b, preferred_element_type=F32)


def _dot_nt(a, b):
    return lax.dot_general(a, b, (((1,), (1,)), ((), ())), preferred_element_type=F32)


def _rms(x, n):
    ms = jnp.sum(x * x, axis=-1, keepdims=True) * (1.0 / n)
    return x * lax.rsqrt(ms + EPS)


def _mm_kernel(x_ref, w_ref, o_ref):
    o_ref[...] = _dot_nt(x_ref[...], w_ref[...]).astype(o_ref.dtype)


def _matmul(x, wt, layer, *, bm, bn, name):
    M, K = x.shape
    N = wt.shape[1]
    bm, bn = min(bm, M), min(bn, N)
    assert M % bm == 0 and N % bn == 0 and wt.shape[2] == K
    w = wt
    return pl.pallas_call(
        _mm_kernel,
        grid=(M // bm, N // bn),
        in_specs=[pl.BlockSpec((bm, K), lambda i, j: (i, 0)),
                  pl.BlockSpec((None, bn, K), lambda i, j: (layer, j, 0))],
        out_specs=pl.BlockSpec((bm, bn), lambda i, j: (i, j)),
        out_shape=jax.ShapeDtypeStruct((M, N), BF16),
        compiler_params=_params("parallel", "arbitrary"),
        name=name,
    )(x, w)


def _conv_kernel(a_ref, g_ref, p_ref, w_ref, b_ref, lg_ref, lb_ref, o_ref, u_ref, *, bt, rt):
    i = pl.program_id(1)

    @pl.when(i == 0)
    def _():
        u_ref[0:CONV_HALO, :] = jnp.zeros((CONV_HALO, u_ref.shape[1]), BF16)

    @pl.when(i > 0)
    def _():
        u_ref[0:CONV_HALO, :] = u_ref[bt:bt + CONV_HALO, :]

    a = a_ref[...].astype(F32)
    g = g_ref[...].astype(F32)
    u_ref[CONV_HALO:CONV_HALO + bt, :] = (a * _sigmoid(g)).astype(BF16)
    bias = b_ref[...]
    sb = p_ref.shape[1] - CONV_LOOKBACK
    for k0 in range(0, bt, sb):
        u_win = u_ref[k0:k0 + sb + CONV_HALO, :]
        sh = [u_win[CONV_HALO - CONV_LOOKBACK:, :].astype(F32)]
        sh += [_dot(p_ref[r - 1], u_win) for r in range(1, SUBLANES)]
        for r0 in range(0, sb, rt):
            acc = [jnp.broadcast_to(bias, (SUBLANES, bias.shape[1]))] * (rt // SUBLANES)
            for j in range(CONV_WIDTH):
                shift = CONV_WIDTH - 1 - j
                r, start = shift % SUBLANES, r0 + CONV_LOOKBACK - (shift // SUBLANES) * SUBLANES
                wj = w_ref[j]
                acc = [acc[q] + wj * sh[r][start + q * SUBLANES:start + (q + 1) * SUBLANES, :]
                       for q in range(rt // SUBLANES)]
            acc = jnp.concatenate(acc, axis=0)
            mu = jnp.mean(acc, axis=-1, keepdims=True)
            d = acc - mu
            var = jnp.mean(d * d, axis=-1, keepdims=True)
            y = d * lax.rsqrt(var + EPS) * lg_ref[...] + lb_ref[...]
            o_ref[k0 + r0:k0 + r0 + rt, :] = _silu(y).astype(o_ref.dtype)


def _shift_matrices(sb):
    m = np.arange(sb + CONV_LOOKBACK)[None, :, None]
    k = np.arange(sb + CONV_HALO)[None, None, :]
    r = np.arange(1, SUBLANES)[:, None, None]
    return jnp.asarray(k == CONV_HALO - CONV_LOOKBACK + m - r, dtype=BF16)


def _conv_branch(proj, col_a, col_g, C, B, S, w, b, ln_g, ln_b, *, bt=256, sb=128, rt=32):
    bt = min(bt, S)
    assert bt >= CONV_HALO and bt % sb == 0 and sb % rt == 0
    nt = S // bt
    ca, cg = col_a // C, col_g // C
    pmat = _shift_matrices(sb)
    w = jnp.broadcast_to(w[:, None, :], (CONV_WIDTH, SUBLANES, C))
    const = lambda b_, i: (0, 0)
    return pl.pallas_call(
        functools.partial(_conv_kernel, bt=bt, rt=rt),
        grid=(B, nt),
        in_specs=[
            pl.BlockSpec((bt, C), lambda b_, i: (b_ * nt + i, ca)),
            pl.BlockSpec((bt, C), lambda b_, i: (b_ * nt + i, cg)),
            pl.BlockSpec(pmat.shape, lambda b_, i: (0, 0, 0)),
            pl.BlockSpec((CONV_WIDTH, SUBLANES, C), lambda b_, i: (0, 0, 0)),
            pl.BlockSpec((1, C), const),
            pl.BlockSpec((1, C), const),
            pl.BlockSpec((1, C), const),
        ],
        out_specs=pl.BlockSpec((bt, C), lambda b_, i: (b_ * nt + i, 0)),
        out_shape=jax.ShapeDtypeStruct((B * S, C), BF16),
        scratch_shapes=[pltpu.VMEM((CONV_HALO + bt, C), BF16)],
        compiler_params=_params("parallel", "arbitrary"),
        name="conformer_conv",
    )(proj, proj, pmat, w, b.reshape(1, C), ln_g.reshape(1, C), ln_b.reshape(1, C))


def _rope_kernel(pos_ref, freq_ref, sgn_ref, cs1_ref, cs2_ref):
    ang = pos_ref[...].astype(F32) * freq_ref[...]
    cs1_ref[...] = jnp.cos(ang)
    cs2_ref[...] = jnp.sin(ang) * sgn_ref[...]


def _rope_tables(pos, *, bm=1024):
    T = pos.shape[0]
    bm = min(bm, T)
    half = ROPE_DIM // 2
    inv_freq = ROPE_BASE ** (-jnp.arange(half, dtype=F32) / half)
    freq = _pad_lanes(jnp.concatenate([inv_freq, inv_freq]))[None, :]
    sgn = _pad_lanes(jnp.concatenate([-jnp.ones((half,), F32), jnp.ones((half,), F32)]))[None, :]
    const = lambda i: (0, 0)
    tab = jax.ShapeDtypeStruct((T, LANES), F32)
    return pl.pallas_call(
        _rope_kernel,
        grid=(T // bm,),
        in_specs=[pl.BlockSpec((bm, 1), lambda i: (i, 0)), pl.BlockSpec((1, LANES), const),
                  pl.BlockSpec((1, LANES), const)],
        out_specs=[pl.BlockSpec((bm, LANES), lambda i: (i, 0)), pl.BlockSpec((bm, LANES), lambda i: (i, 0))],
        out_shape=[tab, tab],
        compiler_params=_params("parallel"),
        name="rope_tables",
    )(pos, freq, sgn)


def _mla_prep_kernel(cq_ref, ckv_ref, kr_ref, krs_ref, cs1_ref, cs2_ref,
                     qn_ref, kvn_ref, wq_ref, wkv_ref, gq_ref, gk_ref,
                     q_ref, k_ref, v_ref, *, scale):
    H = MLA_HEADS
    cs1 = cs1_ref[...]
    cs2 = cs2_ref[...]

    cq = cq_ref[...].astype(F32)
    hq = (_rms(cq, cq.shape[-1]) * qn_ref[...]).astype(BF16)
    qall = _dot(hq, wq_ref[...])
    ckv = ckv_ref[...].astype(F32)
    hkv = (_rms(ckv, ckv.shape[-1]) * kvn_ref[...]).astype(BF16)
    kvall = _dot(hkv, wkv_ref[...])

    g_nope_q, g_rot_q, g_rots_q = gq_ref[0:1, :], gq_ref[1:2, :], gq_ref[2:3, :]
    g_nope_k, g_rot_k, g_rots_k = gk_ref[0:1, :], gk_ref[1:2, :], gk_ref[2:3, :]

    kr = kr_ref[...].astype(F32)
    krs = krs_ref[...].astype(F32)
    rs = lax.rsqrt(jnp.sum(kr * kr, axis=-1, keepdims=True) * (1.0 / ROPE_DIM) + EPS)
    k_rot = ((kr * rs * g_rot_k) * cs1 + (krs * rs * g_rots_k) * cs2).astype(BF16)

    for h in range(H):
        qn = qall[:, h * LANES:(h + 1) * LANES]
        q_ref[h, :, 0:LANES] = (_rms(qn, NOPE_DIM) * g_nope_q * scale).astype(BF16)
        qr = qall[:, (H + h) * LANES:(H + h + 1) * LANES]
        qs = qall[:, (2 * H + h) * LANES:(2 * H + h + 1) * LANES]
        rq = lax.rsqrt(jnp.sum(qr * qr, axis=-1, keepdims=True) * (1.0 / ROPE_DIM) + EPS)
        q_rot = (qr * rq * g_rot_q) * cs1 + (qs * rq * g_rots_q) * cs2
        q_ref[h, :, LANES:2 * LANES] = (q_rot * scale).astype(BF16)
        kn = kvall[:, h * LANES:(h + 1) * LANES]
        k_ref[h, :, 0:LANES] = (_rms(kn, NOPE_DIM) * g_nope_k).astype(BF16)
        k_ref[h, :, LANES:2 * LANES] = k_rot
        v_ref[h, :, 0:V_DIM] = kvall[:, (H + h) * LANES:(H + h + 1) * LANES].astype(BF16)
        v_ref[h, :, V_DIM:V_DIM + LANES] = jnp.ones((kvall.shape[0], LANES), BF16)


def _pad_lanes(v, n=LANES):
    return jnp.pad(v, [(0, 0)] * (v.ndim - 1) + [(0, n - v.shape[-1])])


def _swap_halves(v):
    half = v.shape[-1] // 2
    return jnp.concatenate([v[..., half:], v[..., :half]], axis=-1)


def _mla_prep(proj, cols, cs1, cs2, mla_q_norm, mla_kv_norm, w_q_up, w_kv_up, g_q, g_k, *, bm=512):
    T = proj.shape[0]
    bm = min(bm, T)
    H = MLA_HEADS
    q_lora, kv_lora = w_q_up.shape[0], w_kv_up.shape[0]
    c_cq, c_ckv, c_kr, c_krs = cols
    wq = w_q_up.reshape(q_lora, H, NOPE_DIM + ROPE_DIM)
    wq_rot = wq[:, :, NOPE_DIM:]
    wq_p = jnp.concatenate([
        wq[:, :, :NOPE_DIM].reshape(q_lora, H * NOPE_DIM),
        _pad_lanes(wq_rot).reshape(q_lora, H * LANES),
        _pad_lanes(_swap_halves(wq_rot)).reshape(q_lora, H * LANES)], axis=1).astype(BF16)
    wkv = w_kv_up.reshape(kv_lora, H, NOPE_DIM + V_DIM)
    wkv_p = jnp.concatenate([wkv[:, :, :NOPE_DIM].reshape(kv_lora, H * NOPE_DIM),
                             wkv[:, :, NOPE_DIM:].reshape(kv_lora, H * V_DIM)], axis=1).astype(BF16)

    def gains(g):
        return jnp.stack([g[:NOPE_DIM], _pad_lanes(g[NOPE_DIM:]), _pad_lanes(_swap_halves(g[NOPE_DIM:]))])

    scale = float((NOPE_DIM + ROPE_DIM) ** -0.5 * np.log2(np.e))
    const = lambda i: (0, 0)
    return pl.pallas_call(
        functools.partial(_mla_prep_kernel, scale=scale),
        grid=(T // bm,),
        in_specs=[
            pl.BlockSpec((bm, q_lora), lambda i: (i, c_cq // q_lora)),
            pl.BlockSpec((bm, kv_lora), lambda i: (i, c_ckv // kv_lora)),
            pl.BlockSpec((bm, LANES), lambda i: (i, c_kr // LANES)),
            pl.BlockSpec((bm, LANES), lambda i: (i, c_krs // LANES)),
            pl.BlockSpec((bm, LANES), lambda i: (i, 0)),
            pl.BlockSpec((bm, LANES), lambda i: (i, 0)),
            pl.BlockSpec((1, q_lora), const),
            pl.BlockSpec((1, kv_lora), const),
            pl.BlockSpec(wq_p.shape, const),
            pl.BlockSpec(wkv_p.shape, const),
            pl.BlockSpec((3, LANES), const),
            pl.BlockSpec((3, LANES), const),
        ],
        out_specs=[
            pl.BlockSpec((H, bm, 2 * LANES), lambda i: (0, i, 0)),
            pl.BlockSpec((H, bm, 2 * LANES), lambda i: (0, i, 0)),
            pl.BlockSpec((H, bm, V_DIM + LANES), lambda i: (0, i, 0)),
        ],
        out_shape=[
            jax.ShapeDtypeStruct((H, T, 2 * LANES), BF16),
            jax.ShapeDtypeStruct((H, T, 2 * LANES), BF16),
            jax.ShapeDtypeStruct((H, T, V_DIM + LANES), BF16),
        ],
        compiler_params=_params("parallel"),
        name="mla_prep",
    )(proj, proj, proj, proj, cs1, cs2, mla_q_norm.reshape(1, -1), mla_kv_norm.reshape(1, -1),
      wq_p, wkv_p, gains(g_q), gains(g_k))


def _flash_kernel(q_ref, k_ref, v_ref, o_ref, m_ref, acc_ref, *, bh, nh):
    i = pl.program_id(2)
    m_ref[...] = jnp.full(m_ref.shape, NEG_BIG, F32)
    acc_ref[...] = jnp.zeros(acc_ref.shape, F32)
    nc = bh // LANES

    def chain(a, j, masked):
        rows = slice(a * bh, (a + 1) * bh)
        start = pl.multiple_of(j * bh, bh)
        k = k_ref[0, pl.ds(start, bh), :]
        v = v_ref[0, pl.ds(start, bh), :]
        s = _dot_nt(q_ref[0, rows, :], k)
        if masked:
            row = lax.broadcasted_iota(jnp.int32, s.shape, 0)
            col = lax.broadcasted_iota(jnp.int32, s.shape, 1)
            s = jnp.where(col <= row, s, NEG_BIG)
        chunks = [s[:, c * LANES:(c + 1) * LANES] for c in range(nc)]
        mx = functools.reduce(jnp.maximum, chunks)
        m_prev = m_ref[rows, :]
        m_new = jnp.maximum(m_prev, jnp.max(mx, axis=-1, keepdims=True))
        alpha = jnp.exp2(m_prev - m_new)
        p = jnp.concatenate([jnp.exp2(c - m_new) for c in chunks], axis=1).astype(BF16)
        alpha2 = jnp.concatenate([alpha] * (acc_ref.shape[1] // LANES), axis=1)
        acc_ref[rows, :] = alpha2 * acc_ref[rows, :] + _dot(p, v)
        m_ref[rows, :] = m_new

    def body(j, carry):
        for a in range(nh):
            chain(a, j, False)
        return carry

    lax.fori_loop(0, i * nh, body, 0)
    for a in range(nh):
        for c in range(a + 1):
            chain(a, i * nh + c, c == a)
    o_ref[...] = (acc_ref[:, 0:V_DIM] / acc_ref[:, V_DIM:2 * V_DIM]).astype(o_ref.dtype)


def _flash_attention(q, k, v, B, S, *, bh=512, nh=8):
    H = q.shape[0]
    bh = min(bh, S // nh)
    bq = bh * nh
    nq = S // bq
    return pl.pallas_call(
        functools.partial(_flash_kernel, bh=bh, nh=nh),
        grid=(B, H, nq),
        in_specs=[
            pl.BlockSpec((1, bq, q.shape[2]), lambda b, h, i: (h, b * nq + i, 0)),
            pl.BlockSpec((1, S, k.shape[2]), lambda b, h, i: (h, b, 0)),
            pl.BlockSpec((1, S, v.shape[2]), lambda b, h, i: (h, b, 0)),
        ],
        out_specs=pl.BlockSpec((bq, V_DIM), lambda b, h, i: (b * nq + i, h)),
        out_shape=jax.ShapeDtypeStruct((B * S, H * V_DIM), BF16),
        scratch_shapes=[pltpu.VMEM((bq, LANES), F32), pltpu.VMEM((bq, v.shape[2]), F32)],
        compiler_params=_params("parallel", "parallel", "arbitrary"),
        name="mla_flash",
    )(q, k, v)


def _log_sigmoid(x):
    return -(jnp.maximum(-x, 0.0) + jnp.log(1.0 + jnp.exp(-jnp.abs(x))))


def _mlstm_prep_kernel(xm_ref, halo_ref, cw_ref, cb_ref, wq_ref, wk_ref, wv_ref, wif_ref,
                       bif_ref, q_ref, k_ref, v_ref, xc_ref, gc_ref, gr_ref,
                       buf_ref, qkv_ref, *, bm, L):
    i = pl.program_id(1)
    H = MLSTM_HEADS
    D = xm_ref.shape[1]
    DH = D // H
    halo = halo_ref[...].astype(F32)
    buf_ref[0:MLSTM_HALO, :] = jnp.where(i > 0, halo, 0.0)
    xm_bf = xm_ref[...]
    buf_ref[MLSTM_HALO:MLSTM_HALO + bm, :] = xm_bf.astype(F32)
    off = MLSTM_HALO - (MLSTM_CONV_WIDTH - 1)
    rt = bm // MLSTM_PREP_PARTS
    for r0 in range(0, bm, rt):
        rs = slice(r0, r0 + rt)
        acc = jnp.broadcast_to(cb_ref[...], (rt, D))
        for j in range(MLSTM_CONV_WIDTH):
            acc = acc + cw_ref[j:j + 1, :] * buf_ref[r0 + off + j:r0 + off + j + rt, :]
        xc_bf = _silu(acc).astype(BF16)
        xc_ref[rs, :] = xc_bf
        for h in range(H):
            sl = slice(h * DH, (h + 1) * DH)
            qh = _dot(xc_bf[:, sl], wq_ref[h]).astype(BF16)
            k_raw = _dot(xc_bf[:, sl], wk_ref[h])
            vh = _dot(xm_bf[rs, sl], wv_ref[h]).astype(BF16)
            q_ref[rs, sl] = qh
            k_ref[rs, sl] = (k_raw * (DH ** -0.5)).astype(BF16)
            v_ref[rs, sl] = vh
            qkv_ref[rs, h * DH:(h + 1) * DH] = qh
            qkv_ref[rs, D + h * DH:D + (h + 1) * DH] = k_raw.astype(BF16)
            qkv_ref[rs, 2 * D + h * DH:2 * D + (h + 1) * DH] = vh
    qkv = qkv_ref[...]
    pre_c = _dot(qkv, wif_ref[...]) + bif_ref[...]
    lane = lax.broadcasted_iota(jnp.int32, pre_c.shape, 1)
    is_f_c = (lane >= H) & (lane < 2 * H)
    lf_c = jnp.where(is_f_c, _log_sigmoid(pre_c), 0.0)
    r = lax.broadcasted_iota(jnp.int32, (bm, bm), 0)
    c = lax.broadcasted_iota(jnp.int32, (bm, bm), 1)
    same = (r // L) == (c // L)
    tri = jnp.where(same & (c <= r), 1.0, 0.0).astype(BF16)
    cum_c = _split_dot_left(tri, lf_c)
    gates = jnp.where(is_f_c, cum_c, pre_c)
    gc_ref[...] = gates
    gr_ref[...] = gates.T[0:2 * H, :]


def _split_dot_left(a_exact, b):
    hi = b.astype(BF16)
    lo = (b - hi.astype(F32)).astype(BF16)
    return _dot(a_exact, hi) + _dot(a_exact, lo)


def _mlstm_prep(proj, col_xm, B, S, conv_w, conv_b, w_mq, w_mk, w_mv, w_if, b_if, *, bm=512, L):
    T = B * S
    H = MLSTM_HEADS
    D = conv_w.shape[1]
    bm = min(bm, S)
    assert bm % L == 0
    nt = S // bm
    cxm = col_xm // D
    hb = bm // MLSTM_HALO
    wif = _pad_lanes(w_if).astype(BF16)
    bif = _pad_lanes(b_if)[None, :]
    const2 = lambda b_, i: (0, 0)
    const3 = lambda b_, i: (0, 0, 0)
    row_blk = lambda b_, i: (b_ * nt + i, 0)
    act = jax.ShapeDtypeStruct((T, D), BF16)
    return pl.pallas_call(
        functools.partial(_mlstm_prep_kernel, bm=bm, L=L),
        grid=(B, nt),
        in_specs=[
            pl.BlockSpec((bm, D), lambda b_, i: (b_ * nt + i, cxm)),
            pl.BlockSpec((MLSTM_HALO, D), lambda b_, i: (jnp.maximum((b_ * nt + i) * hb - 1, 0), cxm)),
            pl.BlockSpec((MLSTM_CONV_WIDTH, D), const2),
            pl.BlockSpec((1, D), const2),
            pl.BlockSpec(w_mq.shape, const3),
            pl.BlockSpec(w_mk.shape, const3),
            pl.BlockSpec(w_mv.shape, const3),
            pl.BlockSpec(wif.shape, const2),
            pl.BlockSpec((1, LANES), const2),
        ],
        out_specs=[
            pl.BlockSpec((bm, D), row_blk), pl.BlockSpec((bm, D), row_blk),
            pl.BlockSpec((bm, D), row_blk), pl.BlockSpec((bm, D), row_blk),
            pl.BlockSpec((bm, LANES), row_blk),
            pl.BlockSpec((2 * H, bm), lambda b_, i: (0, b_ * nt + i)),
        ],
        out_shape=[act, act, act, act,
                   jax.ShapeDtypeStruct((T, LANES), F32),
                   jax.ShapeDtypeStruct((2 * H, T), F32)],
        scratch_shapes=[pltpu.VMEM((MLSTM_HALO + bm, D), F32), pltpu.VMEM((bm, 3 * D), BF16)],
        compiler_params=_params("parallel", "arbitrary"),
        name="mlstm_prep",
    )(proj, proj, conv_w, conv_b.reshape(1, D), w_mq.astype(BF16), w_mk.astype(BF16),
      w_mv.astype(BF16), wif, bif)


def _mlstm_scan_kernel(q_ref, k_ref, v_ref, xc_ref, z_ref, gc_ref, gr_ref, gn_ref, skip_ref,
                       o_ref, c_ref, n_ref, m_ref, *, L):
    H = MLSTM_HEADS
    DH = q_ref.shape[1] // H

    @pl.when(pl.program_id(1) == 0)
    def _():
        c_ref[...] = jnp.zeros(c_ref.shape, F32)
        n_ref[...] = jnp.zeros(n_ref.shape, F32)
        m_ref[...] = jnp.zeros(m_ref.shape, F32)

    row = lax.broadcasted_iota(jnp.int32, (L, L), 0)
    col = lax.broadcasted_iota(jnp.int32, (L, L), 1)
    causal = col <= row
    for cc, h in [(cc, h) for cc in range(q_ref.shape[0] // L) for h in range(H)]:
        rs = slice(cc * L, (cc + 1) * L)
        sl = slice(h * DH, (h + 1) * DH)
        q = q_ref[rs, sl]
        k = k_ref[rs, sl]
        v = v_ref[rs, sl]
        li_c = gc_ref[rs, h:h + 1]
        b_c = gc_ref[rs, H + h:H + h + 1]
        li_r = gr_ref[h:h + 1, rs]
        b_r = gr_ref[H + h:H + h + 1, rs]
        m_prev = m_ref[h, 0:1, 0:1]
        C = c_ref[h]
        n = n_ref[h]

        dlog = jnp.where(causal, b_c - b_r + li_r, NEG_BIG)
        inter = b_c + m_prev
        m_comb = jnp.maximum(inter, jnp.max(dlog, axis=-1, keepdims=True))
        w_intra = jnp.exp(dlog - m_comb)
        w_inter = jnp.exp(inter - m_comb)
        s = _dot_nt(q, k) * w_intra
        num = w_inter * _dot(q, C.astype(BF16)) + _dot(s.astype(BF16), v)
        qf = q.astype(F32)
        den = w_inter * jnp.sum(qf * n, axis=-1, keepdims=True) + jnp.sum(s, axis=-1, keepdims=True)
        hh = num / jnp.maximum(jnp.abs(den), jnp.exp(-m_comb))

        b_last = b_c[L - 1:L, :]
        g_c = b_last - b_c + li_c
        m_new = jnp.maximum(b_last + m_prev, jnp.max(g_c, axis=0, keepdims=True))
        wg_c = jnp.exp(g_c - m_new)
        decay = jnp.exp(b_last + m_prev - m_new)
        kw = k.astype(F32) * wg_c
        c_ref[h] = decay * C + _dot(kw.T.astype(BF16), v)
        n_ref[h] = decay * n + jnp.sum(kw, axis=0, keepdims=True)
        m_ref[h] = jnp.broadcast_to(m_new, m_ref.shape[1:])

        mu = jnp.mean(hh, axis=-1, keepdims=True)
        d = hh - mu
        var = jnp.mean(d * d, axis=-1, keepdims=True)
        hn = d * lax.rsqrt(var + EPS) * gn_ref[:, sl]
        hn = hn + skip_ref[:, sl] * xc_ref[rs, sl].astype(F32)
        o_ref[rs, sl] = (hn * _silu(z_ref[rs, sl].astype(F32))).astype(o_ref.dtype)


def _mlstm_scan(q, k, v, xc, proj, col_z, gc, gr, gn_g, skip, B, S, *, L, chunks=2):
    T, D = q.shape
    H = MLSTM_HEADS
    DH = D // H
    R = L * min(chunks, S // L)
    nc = S // R
    cz = col_z // D
    row_blk = lambda b_, i: (b_ * nc + i, 0)
    const2 = lambda b_, i: (0, 0)
    return pl.pallas_call(
        functools.partial(_mlstm_scan_kernel, L=L),
        grid=(B, nc),
        in_specs=[
            pl.BlockSpec((R, D), row_blk), pl.BlockSpec((R, D), row_blk),
            pl.BlockSpec((R, D), row_blk), pl.BlockSpec((R, D), row_blk),
            pl.BlockSpec((R, D), lambda b_, i: (b_ * nc + i, cz)),
            pl.BlockSpec((R, LANES), row_blk),
            pl.BlockSpec((2 * H, R), lambda b_, i: (0, b_ * nc + i)),
            pl.BlockSpec((1, D), const2),
            pl.BlockSpec((1, D), const2),
        ],
        out_specs=pl.BlockSpec((R, D), row_blk),
        out_shape=jax.ShapeDtypeStruct((T, D), BF16),
        scratch_shapes=[pltpu.VMEM((H, DH, DH), F32), pltpu.VMEM((H, 1, DH), F32),
                        pltpu.VMEM((H, 8, LANES), F32)],
        compiler_params=_params("parallel", "arbitrary"),
        name="mlstm_scan",
    )(q, k, v, xc, proj, gc, gr, gn_g.reshape(1, D), skip.reshape(1, D))


def _merge_mix_kernel(c_ref, a_ref, m_ref, g0_ref, g1_ref, g2_ref, bg_ref, wc_ref, wa_ref, wm_ref,
                      wmix_ref, x_ref, o_ref):
    N = x_ref.shape[1]

    def gate(t, g_ref):
        return _sigmoid(g_ref[...].astype(F32) + bg_ref[:, t * N:(t + 1) * N])

    y = gate(0, g0_ref) * _dot(c_ref[...], wc_ref[...])
    y = y + gate(1, g1_ref) * _dot(a_ref[...], wa_ref[...])
    y = y + gate(2, g2_ref) * _dot(m_ref[...], wm_ref[...])
    o_ref[...] = x_ref[...] + _dot(y.astype(BF16), wmix_ref[...])


def _layer_spec(w, layer, **kw):
    return pl.BlockSpec((None,) + w.shape[1:], lambda *idx: (layer, 0, 0), **kw)


def _merge_mix(c, a, m, proj, col_gate, b_gate, wc, wa, wm, wmix, layer, x, *, bm=256):
    T, K = c.shape
    N = wc.shape[2]
    bm = min(bm, T)
    assert col_gate % N == 0
    gcb = col_gate // N
    row = lambda i: (i, 0)
    xs = pl.BlockSpec((bm, K), row)
    once = dict(pipeline_mode=pl.Buffered(1))
    return pl.pallas_call(
        _merge_mix_kernel,
        grid=(T // bm,),
        in_specs=[xs, xs, xs,
                  pl.BlockSpec((bm, N), lambda i: (i, gcb)),
                  pl.BlockSpec((bm, N), lambda i: (i, gcb + 1)),
                  pl.BlockSpec((bm, N), lambda i: (i, gcb + 2)),
                  pl.BlockSpec((1, 3 * N), lambda i: (0, 0)),
                  _layer_spec(wc, layer, **once), _layer_spec(wa, layer, **once),
                  _layer_spec(wm, layer, **once), _layer_spec(wmix, layer, **once),
                  pl.BlockSpec((bm, N), row)],
        out_specs=pl.BlockSpec((bm, N), row),
        out_shape=jax.ShapeDtypeStruct((T, N), F32),
        compiler_params=_params("parallel"),
        name="merge_mix",
    )(c, a, m, proj, proj, proj, b_gate.reshape(1, 3 * N), wc, wa, wm, wmix, x)


def _ffn_kernel(x_ref, g_ref, gn_ref, wg_ref, wu_ref, wo_ref, o_ref, h_ref, *, emit_next):
    j = pl.program_id(1)

    @pl.when(j == 0)
    def _():
        xf = x_ref[...]
        h_ref[...] = (_rms(xf, xf.shape[-1]) * g_ref[...]).astype(BF16)
        o_ref[...] = xf

    h = h_ref[...]
    y = (_silu(_dot(h, wg_ref[...])) * _dot(h, wu_ref[...])).astype(BF16)
    o_ref[...] += _dot(y, wo_ref[...])

    if emit_next:
        @pl.when(j == pl.num_programs(1) - 1)
        def _():
            of = o_ref[...]
            h_ref[...] = (_rms(of, of.shape[-1]) * gn_ref[...]).astype(BF16)


def _ffn(x, gain, w_in, w_out, layer, next_gain=None, *, bm=1024, bh=512):
    T, D = x.shape
    F = w_out.shape[1]
    bm, bh = min(bm, T), min(bh, F)
    nh = F // bh
    emit = next_gain is not None
    gn = (next_gain if emit else gain).reshape(1, D).astype(F32)
    row = pl.BlockSpec((bm, D), lambda i, j: (i, 0))
    vec = pl.BlockSpec((1, D), lambda i, j: (0, 0))
    out_shape = [jax.ShapeDtypeStruct((T, D), F32)]
    out_specs = [row]
    scratch = [pltpu.VMEM((bm, D), BF16)]
    if emit:
        out_shape.append(jax.ShapeDtypeStruct((T, D), BF16))
        out_specs.append(row)
        scratch = []
    res = pl.pallas_call(
        functools.partial(_ffn_kernel, emit_next=emit),
        grid=(T // bm, nh),
        in_specs=[row, vec, vec,
                  pl.BlockSpec((None, D, bh), lambda i, j: (layer, 0, j)),
                  pl.BlockSpec((None, D, bh), lambda i, j: (layer, 0, j + nh)),
                  pl.BlockSpec((None, bh, D), lambda i, j: (layer, j, 0))],
        out_specs=out_specs,
        out_shape=out_shape,
        scratch_shapes=scratch,
        compiler_params=_params("parallel", "arbitrary"),
        name="ffn",
    )(x, gain.reshape(1, D).astype(F32), gn, w_in, w_in, w_out)
    return (res[0], res[1]) if emit else (res[0], None)


def _norm_kernel(x_ref, g_ref, o_ref):
    xf = x_ref[...]
    o_ref[...] = (_rms(xf, xf.shape[-1]) * g_ref[...]).astype(o_ref.dtype)


def _rmsnorm(x, gain, *, bm=1024):
    T, D = x.shape
    bm = min(bm, T)
    return pl.pallas_call(
        _norm_kernel,
        grid=(T // bm,),
        in_specs=[pl.BlockSpec((bm, D), lambda i: (i, 0)), pl.BlockSpec((1, D), lambda i: (0, 0))],
        out_specs=pl.BlockSpec((bm, D), lambda i: (i, 0)),
        out_shape=jax.ShapeDtypeStruct((T, D), BF16),
        compiler_params=_params("parallel"),
        name="rmsnorm",
    )(x, gain.reshape(1, D).astype(F32))


def _mem_kv_kernel(mem_ref, g_ref, w_ref, gk_ref, k_ref, v_ref):
    H, DH = XATTN_HEADS, XATTN_HEAD_DIM
    x = mem_ref[...]
    h = (_rms(x, x.shape[-1]) * g_ref[...]).astype(BF16)
    kv = _dot(h, w_ref[...])
    for hd in range(H):
        kh = kv[:, hd * DH:(hd + 1) * DH]
        k_ref[:, hd * DH:(hd + 1) * DH] = (_rms(kh, DH) * gk_ref[...]).astype(BF16)
    v_ref[...] = kv[:, H * DH:].astype(BF16)


def _mem_kv(mem2d, norm_mem, w_xkv, g_k, *, bm=256):
    M, D = mem2d.shape
    HD = XATTN_HEADS * XATTN_HEAD_DIM
    bm = min(bm, M)
    const = lambda i: (0, 0)
    out = jax.ShapeDtypeStruct((M, HD), BF16)
    return pl.pallas_call(
        _mem_kv_kernel,
        grid=(M // bm,),
        in_specs=[pl.BlockSpec((bm, D), lambda i: (i, 0)), pl.BlockSpec((1, D), const),
                  pl.BlockSpec((D, 2 * HD), const), pl.BlockSpec((1, XATTN_HEAD_DIM), const)],
        out_specs=[pl.BlockSpec((bm, HD), lambda i: (i, 0)), pl.BlockSpec((bm, HD), lambda i: (i, 0))],
        out_shape=[out, out],
        compiler_params=_params("parallel"),
        name="xattn_mem_kv",
    )(mem2d, norm_mem.reshape(1, D), w_xkv.astype(BF16), g_k.reshape(1, -1))


def _xattn_kernel(x_ref, g_ref, wq_ref, gq_ref, k_ref, v_ref, wo_ref, o_ref, att_ref, *, scale):
    H, DH = XATTN_HEADS, XATTN_HEAD_DIM
    x = x_ref[...]
    h = (_rms(x, x.shape[-1]) * g_ref[...]).astype(BF16)
    q = _dot(h, wq_ref[...])
    for hd in range(H):
        sl = slice(hd * DH, (hd + 1) * DH)
        qh = (_rms(q[:, sl], DH) * gq_ref[...] * scale).astype(BF16)
        s = _dot_nt(qh, k_ref[:, sl])
        e = jnp.exp(s - jnp.max(s, axis=-1, keepdims=True))
        o = _dot(e.astype(BF16), v_ref[:, sl]) / jnp.sum(e, axis=-1, keepdims=True)
        att_ref[:, sl] = o.astype(BF16)
    o_ref[...] = x + _dot(att_ref[...], wo_ref[...])


def _xattn(x, norm_x, w_xq, g_q, k, v, w_xo, B, S, *, bm=512):
    T, D = x.shape
    HD = XATTN_HEADS * XATTN_HEAD_DIM
    M = k.shape[0] // B
    bm = min(bm, S)
    nt = S // bm
    const = lambda b_, i: (0, 0)
    return pl.pallas_call(
        functools.partial(_xattn_kernel, scale=float(XATTN_HEAD_DIM ** -0.5)),
        grid=(B, nt),
        in_specs=[
            pl.BlockSpec((bm, D), lambda b_, i: (b_ * nt + i, 0)),
            pl.BlockSpec((1, D), const),
            pl.BlockSpec((D, HD), const),
            pl.BlockSpec((1, XATTN_HEAD_DIM), const),
            pl.BlockSpec((M, HD), lambda b_, i: (b_, 0)),
            pl.BlockSpec((M, HD), lambda b_, i: (b_, 0)),
            pl.BlockSpec((HD, D), const),
        ],
        out_specs=pl.BlockSpec((bm, D), lambda b_, i: (b_ * nt + i, 0)),
        out_shape=jax.ShapeDtypeStruct((T, D), F32),
        scratch_shapes=[pltpu.VMEM((bm, HD), BF16)],
        compiler_params=_params("parallel", "arbitrary"),
        name="xattn",
    )(x, norm_x.reshape(1, D), w_xq.astype(BF16), g_q.reshape(1, -1), k, v, w_xo.astype(BF16))


def _layout_w_in(w_in, C, q_lora, kv_lora, D_ml):
    wt = jnp.swapaxes(w_in, 1, 2)
    n_head = 2 * C + q_lora + kv_lora + ROPE_DIM
    o_kr = n_head - ROPE_DIM
    half = ROPE_DIM // 2
    zpad = jnp.zeros((wt.shape[0], LANES - ROPE_DIM, wt.shape[2]), BF16)
    tail = wt[:, n_head:, :].astype(BF16)
    head = jnp.concatenate([wt[:, :n_head, :].astype(BF16), zpad, wt[:, o_kr + half:n_head, :].astype(BF16),
                            wt[:, o_kr:o_kr + half, :].astype(BF16), zpad], axis=1)
    cols = dict(a=0, g=C, cq=2 * C, ckv=2 * C + q_lora, kr=o_kr, krs=o_kr + LANES,
                xm=0, z=D_ml, gate=2 * D_ml)
    return head, tail, cols


def kernel(x, mem, positions, norm_mix, w_in, b_gate, conv_dw, conv_dw_b, conv_ln_g, conv_ln_b, w_conv_out, mla_q_norm, mla_kv_norm, w_q_up, w_kv_up, mla_g_q, mla_g_k, w_mla_out, mlstm_conv_w, mlstm_conv_b, w_mq, w_mk, w_mv, w_if, b_if, mlstm_gn_g, mlstm_skip, w_mlstm_out, w_mix_out, norm_x, norm_mem, w_xq, w_xkv, xattn_g_q, xattn_g_k, w_xo, norm_ffn, w_ffn_in, w_ffn_out):
    B, S, D = x.shape
    T = B * S
    depth = w_in.shape[0]
    C = conv_dw.shape[2]
    D_ml = mlstm_conv_w.shape[2]
    xt = x.reshape(T, D)
    mem2d = mem.reshape(-1, D)
    cs1, cs2 = _rope_tables(positions.reshape(T, 1))
    w_head, w_tail, cols = _layout_w_in(w_in, C, w_q_up.shape[1], w_kv_up.shape[1], D_ml)
    w_conv_out, w_mla_out, w_mlstm_out, w_mix_out, w_ffn_in, w_ffn_out = (
        w.astype(BF16) for w in (w_conv_out, w_mla_out, w_mlstm_out, w_mix_out, w_ffn_in, w_ffn_out))
    h = _rmsnorm(xt, norm_mix[0])
    for l in range(depth):
        L = min(MLSTM_CHUNK, S)
        ph = _matmul(h, w_head, l, bm=2048, bn=1024, name="in_proj_head")
        pt = _matmul(h, w_tail, l, bm=2048, bn=1024, name="in_proj_tail")
        c = _conv_branch(ph, cols["a"], cols["g"], C, B, S, conv_dw[l], conv_dw_b[l],
                         conv_ln_g[l], conv_ln_b[l])
        q, k, v = _mla_prep(ph, (cols["cq"], cols["ckv"], cols["kr"], cols["krs"]), cs1, cs2,
                            mla_q_norm[l], mla_kv_norm[l], w_q_up[l], w_kv_up[l], mla_g_q[l], mla_g_k[l])
        a = _flash_attention(q, k, v, B, S)
        mq, mk, mv, xc, gc, gr = _mlstm_prep(pt, cols["xm"], B, S, mlstm_conv_w[l], mlstm_conv_b[l],
                                             w_mq[l], w_mk[l], w_mv[l], w_if[l], b_if[l], L=L)
        m = _mlstm_scan(mq, mk, mv, xc, pt, cols["z"], gc, gr, mlstm_gn_g[l], mlstm_skip[l], B, S, L=L)
        xt = _merge_mix(c, a, m, pt, cols["gate"], b_gate[l], w_conv_out, w_mla_out,
                        w_mlstm_out, w_mix_out, l, xt)
        xk, xv = _mem_kv(mem2d, norm_mem[l], w_xkv[l], xattn_g_k[l])
        xt = _xattn(xt, norm_x[l], w_xq[l], xattn_g_q[l], xk, xv, w_xo[l], B, S)
        xt, h = _ffn(xt, norm_ffn[l], w_ffn_in, w_ffn_out, l,
                     next_gain=norm_mix[l + 1] if l + 1 < depth else None)
    return xt.reshape(B, S, D)
```
